```python
import functools
import jax, jax.numpy as jnp
from jax import lax
import numpy as np

D_MODEL = 1024
BATCH = 8
SEQ = 2048
DEPTH = 4
DEC_BATCH = 128
DEC_SEQ = 4
PAST_LEN = 8192
PAGE_SIZE = 128

HEAD_DIM = 64
MIX_W = D_MODEL
MLA_HEADS = 8
MLA_NOPE = 64
MLA_ROPE = 32
MLA_V = 64
MLA_W = MLA_HEADS * MLA_V
KV_LORA = 256
Q_LORA = 384
CONV_W = 256
CONV_WIDTH = 3
MEM_HEADS = 4
MEM_W = MEM_HEADS * HEAD_DIM
MEM_TOKENS = 256
D_FF = 2816
Q_BLOCK = 128
ROPE_THETA = 10000.0
EPS = 1e-6
MLA_SCALE = (MLA_NOPE + MLA_ROPE) ** -0.5
MEM_SCALE = HEAD_DIM ** -0.5
NEG = -1e30
IN_COLS = Q_LORA + KV_LORA + MLA_ROPE + 3 * CONV_W + MEM_W

kernel_name = 'hybrid_mla_shortconv_mem_convffn_step'


def _rmsnorm(x, g):
    xf = x.astype(jnp.float32)
    y = xf * lax.rsqrt(jnp.mean(xf * xf, axis=-1, keepdims=True) + EPS) * g.astype(jnp.float32)
    return y.astype(x.dtype)


def _rope(x, pos):
    half = MLA_ROPE // 2
    freqs = ROPE_THETA ** (-jnp.arange(half, dtype=jnp.float32) * (2.0 / MLA_ROPE))
    ang = pos.astype(jnp.float32)[:, None] * freqs[None, :]
    shape = (1, ang.shape[0]) + (1,) * (x.ndim - 3) + (half,)
    cos = jnp.cos(ang).reshape(shape)
    sin = jnp.sin(ang).reshape(shape)
    xf = x.astype(jnp.float32)
    x1, x2 = xf[..., :half], xf[..., half:]
    return jnp.concatenate([x1 * cos - x2 * sin, x1 * sin + x2 * cos], axis=-1).astype(x.dtype)


def _causal_conv3(u, prev, w):
    full = jnp.concatenate([prev, u], axis=1)
    out = w[0] * full[:, :-2] + w[1] * full[:, 1:-1] + w[2] * full[:, 2:]
    return out, full[:, -(CONV_WIDTH - 1):]


def _mla_attend(q_lat, q_pe, ckv, kpe, q_pos, k_pos):
    s = (jnp.einsum('bthr,bkr->bhtk', q_lat, ckv) + jnp.einsum('bthp,bkp->bhtk', q_pe, kpe)).astype(jnp.float32) * MLA_SCALE
    mask = k_pos[None, :] <= q_pos[:, None]
    s = jnp.where(mask[None, None], s, NEG)
    p = jax.nn.softmax(s, axis=-1).astype(ckv.dtype)
    return jnp.einsum('bhtk,bkr->bthr', p, ckv)


def _prompt_attend(q_lat, q_pe, ckv, kpe):
    B, T, H, R = q_lat.shape
    nb = T // Q_BLOCK
    pos = jnp.arange(T, dtype=jnp.int32)
    qb = q_lat.reshape(B, nb, Q_BLOCK, H, R).transpose(1, 0, 2, 3, 4)
    pb = q_pe.reshape(B, nb, Q_BLOCK, H, MLA_ROPE).transpose(1, 0, 2, 3, 4)
    posb = pos.reshape(nb, Q_BLOCK)
    out = lax.map(lambda a: _mla_attend(a[0], a[1], ckv, kpe, a[2], pos), (qb, pb, posb))
    return out.transpose(1, 0, 2, 3, 4).reshape(B, T, H, R)


def _sample_attend(q_lat, q_pe, ckv, kpe, pool_ckv, pool_kpe, page_table):
    DB, T = q_lat.shape[:2]
    past_ckv = pool_ckv[page_table].reshape(DB, -1, KV_LORA)
    past_kpe = pool_kpe[page_table].reshape(DB, -1, MLA_ROPE)
    P = past_ckv.shape[1]
    keys_c = jnp.concatenate([past_ckv, ckv], axis=1)
    keys_p = jnp.concatenate([past_kpe, kpe], axis=1)
    k_pos = jnp.arange(P + T, dtype=jnp.int32)
    q_pos = P + jnp.arange(T, dtype=jnp.int32)
    return _mla_attend(q_lat, q_pe, keys_c, keys_p, q_pos, k_pos)


def _mem_kv(mem, g, w_k, w_v):
    B, M, _ = mem.shape
    m = _rmsnorm(mem, g)
    return (m @ w_k).reshape(B, M, MEM_HEADS, HEAD_DIM), (m @ w_v).reshape(B, M, MEM_HEADS, HEAD_DIM)


def _layer(x, pos, attend, mem_k, mem_v, conv_prev, ffn_prev,
           norm_pre_mix, w_in, norm_q, w_uq, norm_kv, w_uk, w_uv, conv_w, norm_group, w_out,
           norm_post_mix, norm_pre_ffn, w_gate, w_up, ffn_conv_w, ffn_conv_b, w_down, norm_post_ffn):
    B, T, _ = x.shape
    h = _rmsnorm(x, norm_pre_mix)
    z = h @ w_in
    o0 = Q_LORA
    o1 = o0 + KV_LORA
    o2 = o1 + MLA_ROPE
    o3 = o2 + CONV_W
    o4 = o3 + CONV_W
    o5 = o4 + CONV_W
    cq, ckv_raw, kpe_raw = z[..., :o0], z[..., o0:o1], z[..., o1:o2]
    gb, gc, hv, qm = z[..., o2:o3], z[..., o3:o4], z[..., o4:o5], z[..., o5:]

    q = (_rmsnorm(cq, norm_q) @ w_uq).reshape(B, T, MLA_HEADS, MLA_NOPE + MLA_ROPE)
    q_nope, q_pe = q[..., :MLA_NOPE], _rope(q[..., MLA_NOPE:], pos)
    ckv = _rmsnorm(ckv_raw, norm_kv)
    kpe = _rope(kpe_raw, pos)
    q_lat = jnp.einsum('bthn,rhn->bthr', q_nope, w_uk)
    o_lat = attend(q_lat, q_pe, ckv, kpe)
    o_mla = jnp.einsum('bthr,rhv->bthv', o_lat, w_uv).reshape(B, T, MLA_W)

    cv, conv_new = _causal_conv3(gc * hv, conv_prev, conv_w)
    o_conv = gb * cv

    qh = qm.reshape(B, T, MEM_HEADS, HEAD_DIM)
    s = jnp.einsum('bthd,bmhd->bhtm', qh, mem_k).astype(jnp.float32) * MEM_SCALE
    p = jax.nn.softmax(s, axis=-1).astype(mem_v.dtype)
    o_mem = jnp.einsum('bhtm,bmhd->bthd', p, mem_v).reshape(B, T, MEM_W)

    merged = jnp.concatenate([
        _rmsnorm(o_mla, norm_group[:MLA_W]),
        _rmsnorm(o_conv, norm_group[MLA_W:MLA_W + CONV_W]),
        _rmsnorm(o_mem, norm_group[MLA_W + CONV_W:])], axis=-1)
    x = x + _rmsnorm(merged @ w_out, norm_post_mix)

    h2 = _rmsnorm(x, norm_pre_ffn)
    g, ffn_new = _causal_conv3(h2 @ w_gate, ffn_prev, ffn_conv_w)
    a = jax.nn.silu(g + ffn_conv_b) * (h2 @ w_up)
    x = x + _rmsnorm(a @ w_down, norm_post_ffn)
    return x, ckv, kpe, conv_new, ffn_new


def setup_inputs(seed: int = 0) -> dict:
    key = jax.random.key(seed)
    ks = jax.random.split(key, 40)
    f32 = jnp.float32
    n_pages = PAST_LEN // PAGE_SIZE
    used = DEC_BATCH * n_pages
    n_phys = used + max(1, used // 4)

    def nrm(k, shape, scale=1.0):
        return jax.random.normal(k, shape, f32) * scale

    def gain(k, shape):
        return 1.0 + 0.02 * jax.random.normal(k, shape, f32)

    page_table = jax.random.permutation(ks[0], n_phys)[:used].reshape(DEC_BATCH, n_pages).astype(jnp.int32)
    return {
        'x_prompt': nrm(ks[1], (BATCH, SEQ, D_MODEL)),
        'x_sample': nrm(ks[2], (DEC_BATCH, DEC_SEQ, D_MODEL)),
        'mem_prompt': nrm(ks[3], (BATCH, MEM_TOKENS, D_MODEL)),
        'cache_ckv': nrm(ks[4], (DEPTH, n_phys, PAGE_SIZE, KV_LORA)),
        'cache_kpe': nrm(ks[5], (DEPTH, n_phys, PAGE_SIZE, MLA_ROPE)),
        'cache_mem_k': nrm(ks[6], (DEPTH, DEC_BATCH, MEM_TOKENS, MEM_HEADS, HEAD_DIM)),
        'cache_mem_v': nrm(ks[7], (DEPTH, DEC_BATCH, MEM_TOKENS, MEM_HEADS, HEAD_DIM)),
        'state_conv': nrm(ks[8], (DEPTH, DEC_BATCH, CONV_WIDTH - 1, CONV_W)),
        'state_ffn_conv': nrm(ks[9], (DEPTH, DEC_BATCH, CONV_WIDTH - 1, D_FF)),
        'page_table': page_table,
        'norm_pre_mix': gain(ks[10], (DEPTH, D_MODEL)),
        'w_in': nrm(ks[11], (DEPTH, D_MODEL, IN_COLS), D_MODEL ** -0.5),
        'norm_q': gain(ks[12], (DEPTH, Q_LORA)),
        'w_uq': nrm(ks[13], (DEPTH, Q_LORA, MLA_HEADS * (MLA_NOPE + MLA_ROPE)), Q_LORA ** -0.5),
        'norm_kv': gain(ks[14], (DEPTH, KV_LORA)),
        'w_uk': nrm(ks[15], (DEPTH, KV_LORA, MLA_HEADS, MLA_NOPE), KV_LORA ** -0.5),
        'w_uv': nrm(ks[16], (DEPTH, KV_LORA, MLA_HEADS, MLA_V), KV_LORA ** -0.5),
        'conv_w': nrm(ks[17], (DEPTH, CONV_WIDTH, CONV_W), CONV_WIDTH ** -0.5),
        'norm_mem': gain(ks[18], (DEPTH, D_MODEL)),
        'w_mem_k': nrm(ks[19], (DEPTH, D_MODEL, MEM_W), D_MODEL ** -0.5),
        'w_mem_v': nrm(ks[20], (DEPTH, D_MODEL, MEM_W), D_MODEL ** -0.5),
        'norm_group': gain(ks[21], (DEPTH, MIX_W)),
        'w_out': nrm(ks[22], (DEPTH, MIX_W, D_MODEL), MIX_W ** -0.5),
        'norm_post_mix': gain(ks[23], (DEPTH, D_MODEL)),
        'norm_pre_ffn': gain(ks[24], (DEPTH, D_MODEL)),
        'w_gate': nrm(ks[25], (DEPTH, D_MODEL, D_FF), D_MODEL ** -0.5),
        'w_up': nrm(ks[26], (DEPTH, D_MODEL, D_FF), D_MODEL ** -0.5),
        'ffn_conv_w': nrm(ks[27], (DEPTH, CONV_WIDTH, D_FF), CONV_WIDTH ** -0.5),
        'ffn_conv_b': nrm(ks[28], (DEPTH, D_FF), 0.02),
        'w_down': nrm(ks[29], (DEPTH, D_FF, D_MODEL), D_FF ** -0.5),
        'norm_post_ffn': gain(ks[30], (DEPTH, D_MODEL)),
    }


def reference(x_prompt, x_sample, mem_prompt, cache_ckv, cache_kpe, cache_mem_k, cache_mem_v,
              state_conv, state_ffn_conv, page_table,
              norm_pre_mix, w_in, norm_q, w_uq, norm_kv, w_uk, w_uv, conv_w, norm_mem, w_mem_k, w_mem_v,
              norm_group, w_out, norm_post_mix, norm_pre_ffn, w_gate, w_up, ffn_conv_w, ffn_conv_b,
              w_down, norm_post_ffn):
    B, S, _ = x_prompt.shape
    T = x_sample.shape[1]
    past = page_table.shape[1] * cache_ckv.shape[2]
    pos_p = jnp.arange(S, dtype=jnp.int32)
    pos_s = past + jnp.arange(T, dtype=jnp.int32)
    conv0 = jnp.zeros((B, CONV_WIDTH - 1, CONV_W), x_prompt.dtype)
    ffn0 = jnp.zeros((B, CONV_WIDTH - 1, D_FF), x_prompt.dtype)

    yp, ys = x_prompt, x_sample
    p_ckv, p_kpe, p_mk, p_mv, p_conv, p_ffn = [], [], [], [], [], []
    s_ckv, s_kpe, s_conv, s_ffn = [], [], [], []
    for l in range(DEPTH):
        lw = (norm_pre_mix[l], w_in[l], norm_q[l], w_uq[l], norm_kv[l], w_uk[l], w_uv[l], conv_w[l],
              norm_group[l], w_out[l], norm_post_mix[l], norm_pre_ffn[l], w_gate[l], w_up[l],
              ffn_conv_w[l], ffn_conv_b[l], w_down[l], norm_post_ffn[l])
        mk, mv = _mem_kv(mem_prompt, norm_mem[l], w_mem_k[l], w_mem_v[l])
        yp, ckv, kpe, cn, fn = _layer(yp, pos_p, _prompt_attend, mk, mv, conv0, ffn0, *lw)
        p_ckv.append(ckv)
        p_kpe.append(kpe)
        p_mk.append(mk)
        p_mv.append(mv)
        p_conv.append(cn)
        p_ffn.append(fn)
        att = functools.partial(_sample_attend, pool_ckv=cache_ckv[l], pool_kpe=cache_kpe[l], page_table=page_table)
        ys, ckv2, kpe2, cn2, fn2 = _layer(ys, pos_s, att, cache_mem_k[l], cache_mem_v[l],
                                         state_conv[l], state_ffn_conv[l], *lw)
        s_ckv.append(ckv2)
        s_kpe.append(kpe2)
        s_conv.append(cn2)
        s_ffn.append(fn2)

    return (yp, ys,
            jnp.stack(p_ckv), jnp.stack(p_kpe), jnp.stack(p_mk), jnp.stack(p_mv),
            jnp.stack(p_conv), jnp.stack(p_ffn),
            jnp.stack(s_ckv), jnp.stack(s_kpe), jnp.stack(s_conv), jnp.stack(s_ffn))
```

```python
import functools

import jax
import jax.numpy as jnp
from jax import lax
from jax.experimental import pallas as pl
from jax.experimental.pallas import tpu as pltpu

F32 = jnp.float32
BF16 = jnp.bfloat16

D_MODEL = 1024
DEPTH = 4
HEAD_DIM = 64
MLA_HEADS = 8
MLA_NOPE = 64
MLA_ROPE = 32
MLA_V = 64
MLA_W = MLA_HEADS * MLA_V
KV_LORA = 256
Q_LORA = 384
CONV_W = 256
MEM_HEADS = 4
MEM_W = MEM_HEADS * HEAD_DIM
MEM_TOKENS = 256
D_FF = 2816
ROPE_THETA = 10000.0
EPS = 1e-6
MLA_SCALE = (MLA_NOPE + MLA_ROPE) ** -0.5
MEM_SCALE = HEAD_DIM ** -0.5
NEG = -1e30

Z_CQ = 0
Z_CKV = Z_CQ + Q_LORA
Z_GB = Z_CKV + KV_LORA
Z_GC = Z_GB + CONV_W
Z_HV = Z_GC + CONV_W
Z_QM = Z_HV + CONV_W
Z_KPE = Z_QM + MEM_W
Z_COLS = Z_KPE + 128

SUBLANES = 8
VMEM_LIMIT_BYTES = 56 * 1024 * 1024

PROMPT_TM = 256
ATTN_TQ = 128
ATTN_TK = 512
PAGE_CHUNK = 2048
NEW_PAD = 16
SMEM_BLOCK = 8


def _rms(x, g):
    return x * lax.rsqrt(jnp.mean(x * x, axis=-1, keepdims=True) + EPS) * g


def _dot(a, b):
    return jnp.dot(a, b, preferred_element_type=F32)


def _dot_nt(a, b):
    return lax.dot_general(a, b, (((1,), (1,)), ((), ())), preferred_element_type=F32)


def _resident(shape, index_map):
    return pl.BlockSpec(shape, index_map, pipeline_mode=pl.Buffered(1))


def _params(*sem):
    return pltpu.CompilerParams(dimension_semantics=sem, vmem_limit_bytes=VMEM_LIMIT_BYTES)


def _head_mask(lane, h, width):
    return (lane >= width * h) & (lane < width * (h + 1))


def _carry_base(s):
    return -(-2 * s // SUBLANES) * SUBLANES


def _mem_kv_kernel(mem_ref, g_ref, wk_ref, wv_ref, k_ref, v_ref):
    m = _rms(mem_ref[0], g_ref[0]).astype(BF16)
    k_ref[0, 0] = _dot(m, wk_ref[0])
    v_ref[0, 0] = _dot(m, wv_ref[0])


def _mem_kv(mem, norm_mem, wk, wv):
    nb = mem.shape[0]
    out = jax.ShapeDtypeStruct((DEPTH, nb, MEM_TOKENS, MEM_W), F32)
    return pl.pallas_call(
        _mem_kv_kernel,
        grid=(DEPTH, nb),
        in_specs=[
            pl.BlockSpec((1, MEM_TOKENS, D_MODEL), lambda l, b: (b, 0, 0)),
            pl.BlockSpec((1, 1, D_MODEL), lambda l, b: (l, 0, 0)),
            pl.BlockSpec((1, D_MODEL, MEM_W), lambda l, b: (l, 0, 0)),
            pl.BlockSpec((1, D_MODEL, MEM_W), lambda l, b: (l, 0, 0)),
        ],
        out_specs=[
            pl.BlockSpec((1, 1, MEM_TOKENS, MEM_W), lambda l, b: (l, b, 0, 0)),
            pl.BlockSpec((1, 1, MEM_TOKENS, MEM_W), lambda l, b: (l, b, 0, 0)),
        ],
        out_shape=[out, out],
        compiler_params=_params("parallel", "parallel"),
        name="mem_kv",
    )(mem, norm_mem, wk, wv)


def _mem_attend(qm, mk, mv, rows):
    lane = lax.broadcasted_iota(jnp.int32, (rows, MEM_W), 1)
    masks = [_head_mask(lane, h, HEAD_DIM) for h in range(MEM_HEADS)]
    qs = jnp.concatenate([jnp.where(m, qm, 0.0) for m in masks], axis=0).astype(BF16)
    sc = _dot_nt(qs, mk.astype(BF16)) * MEM_SCALE
    p = jnp.exp(sc - jnp.max(sc, axis=-1, keepdims=True))
    o = _dot(p.astype(BF16), mv.astype(BF16)) / jnp.sum(p, axis=-1, keepdims=True)
    om = jnp.where(masks[0], o[0:rows], 0.0)
    for h in range(1, MEM_HEADS):
        om = om + jnp.where(masks[h], o[h * rows:(h + 1) * rows], 0.0)
    return om


def _in_proj_kernel(*refs, tm, s, has_carry, do_mem):
    it = iter(refs)
    (x_ref, cosq_ref, sinq_ref, tabk_ref, npm_ref, win_ref, nq_ref, wuq_ref, nkv_ref,
     wuk_ref, cw_ref, ng_ref) = [next(it) for _ in range(12)]
    c0_ref = next(it) if has_carry else None
    if do_mem:
        mk_ref, mv_ref = next(it), next(it)
    qlat_ref, qpe_ref, kall_ref, ckv_ref, kpe_ref = [next(it) for _ in range(5)]
    if do_mem:
        mbc_ref = next(it)
    else:
        mb_ref, qm_ref = next(it), next(it)
    cnew_ref = next(it)
    ubuf = next(it)

    i = pl.program_id(1)
    base = _carry_base(s)

    h = _rms(x_ref[...], npm_ref[0]).astype(BF16)
    z = _dot(h, win_ref[0])

    cqn = _rms(z[:, Z_CQ:Z_CKV], nq_ref[0]).astype(BF16)
    q = _dot(cqn, wuq_ref[0])
    qpe = q[:, 512:768] * cosq_ref[...] + q[:, 768:1024] * sinq_ref[...]
    qpe_ref[0] = qpe.astype(BF16)
    for j in range(MLA_HEADS // 2):
        ql = _dot(q[:, 128 * j:128 * (j + 1)].astype(BF16), wuk_ref[0, j])
        qlat_ref[0, 2 * j] = ql[:, :KV_LORA].astype(BF16)
        qlat_ref[0, 2 * j + 1] = ql[:, KV_LORA:].astype(BF16)

    ckv = _rms(z[:, Z_CKV:Z_GB], nkv_ref[0])
    ckv_ref[...] = ckv
    t = z[:, Z_KPE:Z_COLS] * tabk_ref[...]
    r = t + pltpu.roll(t, 96, 1)
    lane = lax.broadcasted_iota(jnp.int32, (tm, 128), 1)
    kp = jnp.where(lane < MLA_ROPE, r, 0.0)
    kpe_ref[...] = kp[:, :MLA_ROPE]
    kp = kp + pltpu.roll(kp, 32, 1)
    kp = (kp + pltpu.roll(kp, 64, 1)).astype(BF16)
    kall_ref[0, :, 0:KV_LORA] = ckv.astype(BF16)
    kall_ref[0, :, KV_LORA:KV_LORA + 128] = kp
    kall_ref[0, :, KV_LORA + 128:KV_LORA + 256] = kp

    u = z[:, Z_GC:Z_HV] * z[:, Z_HV:Z_QM]

    @pl.when(i == 0)
    def _():
        if has_carry:
            ubuf[base - 2 * s:base, :] = c0_ref[0]
        else:
            ubuf[0:base, :] = jnp.zeros((base, CONV_W), F32)

    ubuf[base:base + tm, :] = u
    w = cw_ref[0]
    cv = (w[0:1] * ubuf[base - 2 * s:base - 2 * s + tm, :]
          + w[1:2] * ubuf[base - s:base - s + tm, :] + w[2:3] * u)
    newc = ubuf[base + tm - 2 * s:base + tm, :]
    ubuf[base - 2 * s:base, :] = newc
    cnew_ref[0] = newc
    ng = ng_ref[0]
    mb = _rms(z[:, Z_GB:Z_GC] * cv, ng[:, MLA_W:MLA_W + CONV_W]).astype(BF16)

    qm = z[:, Z_QM:Z_KPE]
    if do_mem:
        om = _mem_attend(qm, mk_ref[0, 0], mv_ref[0, 0], tm)
        mbc_ref[:, 0:CONV_W] = mb
        mbc_ref[:, CONV_W:] = _rms(om, ng[:, MLA_W + CONV_W:]).astype(BF16)
    else:
        mb_ref[...] = mb
        qm_ref[...] = qm


def _in_proj(l, x, nb, t_len, tm, s, tabs, wts, carry0=None, mem_kv=None):
    nt = t_len // tm
    rows = nb * t_len
    has_carry = carry0 is not None
    do_mem = mem_kv is not None
    row_map = lambda b, i: (b * nt + i, 0)
    tab_map = lambda b, i: (i, 0)
    lay = lambda b, i: (l, 0, 0)
    in_specs = [
        pl.BlockSpec((tm, D_MODEL), row_map),
        pl.BlockSpec((tm, 256), tab_map),
        pl.BlockSpec((tm, 256), tab_map),
        pl.BlockSpec((tm, 128), tab_map),
        pl.BlockSpec((1, 1, D_MODEL), lay),
        _resident((1, D_MODEL, Z_COLS), lay),
        pl.BlockSpec((1, 1, Q_LORA), lay),
        _resident((1, Q_LORA, 1024), lay),
        pl.BlockSpec((1, 1, KV_LORA), lay),
        _resident((1, MLA_HEADS // 2, 128, 2 * KV_LORA), lambda b, i: (l, 0, 0, 0)),
        pl.BlockSpec((1, 3, CONV_W), lay),
        pl.BlockSpec((1, 1, D_MODEL), lay),
    ]
    args = [x, *tabs, wts["norm_pre_mix"], wts["w_in"], wts["norm_q"], wts["w_uq"],
            wts["norm_kv"], wts["w_uk2"], wts["conv_w"], wts["norm_group"]]
    if has_carry:
        in_specs.append(pl.BlockSpec((1, 2 * s, CONV_W), lambda b, i: (b, 0, 0)))
        args.append(carry0)
    if do_mem:
        mspec = pl.BlockSpec((1, 1, MEM_TOKENS, MEM_W), lambda b, i: (l, b, 0, 0))
        in_specs += [mspec, mspec]
        args += list(mem_kv)
    out_shape = [
        jax.ShapeDtypeStruct((nb, MLA_HEADS, t_len, KV_LORA), BF16),
        jax.ShapeDtypeStruct((nb, t_len, 256), BF16),
        jax.ShapeDtypeStruct((nb, t_len, 512), BF16),
        jax.ShapeDtypeStruct((rows, KV_LORA), F32),
        jax.ShapeDtypeStruct((rows, MLA_ROPE), F32),
    ]
    out_specs = [
        pl.BlockSpec((1, MLA_HEADS, tm, KV_LORA), lambda b, i: (b, 0, i, 0)),
        pl.BlockSpec((1, tm, 256), lambda b, i: (b, i, 0)),
        pl.BlockSpec((1, tm, 512), lambda b, i: (b, i, 0)),
        pl.BlockSpec((tm, KV_LORA), row_map),
        pl.BlockSpec((tm, MLA_ROPE), row_map),
    ]
    if do_mem:
        out_shape.append(jax.ShapeDtypeStruct((rows, 512), BF16))
        out_specs.append(pl.BlockSpec((tm, 512), row_map))
    else:
        out_shape += [jax.ShapeDtypeStruct((rows, CONV_W), BF16),
                      jax.ShapeDtypeStruct((rows, MEM_W), F32)]
        out_specs += [pl.BlockSpec((tm, CONV_W), row_map), pl.BlockSpec((tm, MEM_W), row_map)]
    out_shape.append(jax.ShapeDtypeStruct((nb, 2 * s, CONV_W), F32))
    out_specs.append(pl.BlockSpec((1, 2 * s, CONV_W), lambda b, i: (b, 0, 0)))
    kern = functools.partial(_in_proj_kernel, tm=tm, s=s, has_carry=has_carry, do_mem=do_mem)
    return pl.pallas_call(
        kern,
        grid=(nb, nt),
        in_specs=in_specs,
        out_specs=out_specs,
        out_shape=out_shape,
        scratch_shapes=[pltpu.VMEM((_carry_base(s) + tm, CONV_W), F32)],
        compiler_params=_params("parallel", "arbitrary"),
        name="in_proj_mem" if do_mem else "in_proj",
    )(*args)


def _attn_kernel(qlat_ref, qpe_ref, kall_ref, wuv_ref, ng_ref, o_ref, q2, m_scr, l_scr, acc,
                 *, tq, tk):
    i = pl.program_id(1)
    k = pl.program_id(2)
    last_k = (i * tq + tq - 1) // tk
    rows = MLA_HEADS * tq

    @pl.when(k == 0)
    def _():
        lane = lax.broadcasted_iota(jnp.int32, (tq, 256), 1)
        qp = qpe_ref[0]
        zero = jnp.zeros_like(qp)
        for h in range(MLA_HEADS):
            q2[h * tq:(h + 1) * tq, 0:KV_LORA] = qlat_ref[0, h]
            q2[h * tq:(h + 1) * tq, KV_LORA:] = jnp.where(_head_mask(lane, h, MLA_ROPE), qp, zero)
        m_scr[...] = jnp.full((rows, 1), NEG, F32)
        l_scr[...] = jnp.zeros((rows, 1), F32)
        acc[...] = jnp.zeros((rows, KV_LORA), F32)

    @pl.when(k <= last_k)
    def _():
        kk = kall_ref[0]
        sc = _dot_nt(q2[...], kk) * MLA_SCALE
        q_pos = i * tq + (lax.broadcasted_iota(jnp.int32, (rows, tk), 0) & (tq - 1))
        k_pos = k * tk + lax.broadcasted_iota(jnp.int32, (rows, tk), 1)
        sc = jnp.where(k_pos <= q_pos, sc, NEG)
        m_prev = m_scr[...]
        m_new = jnp.maximum(m_prev, jnp.max(sc, axis=-1, keepdims=True))
        alpha = jnp.exp(m_prev - m_new)
        p = jnp.exp(sc - m_new)
        l_scr[...] = alpha * l_scr[...] + jnp.sum(p, axis=-1, keepdims=True)
        acc[...] = alpha * acc[...] + _dot(p.astype(BF16), kk[:, 0:KV_LORA])
        m_scr[...] = m_new

    @pl.when(k == last_k)
    def _():
        o = (acc[...] / l_scr[...]).astype(BF16)
        om = _dot(o[0:tq], wuv_ref[0, 0])
        for h in range(1, MLA_HEADS):
            om = om + _dot(o[h * tq:(h + 1) * tq], wuv_ref[0, h])
        o_ref[...] = _rms(om, ng_ref[0][:, 0:MLA_W]).astype(BF16)


def _prompt_attention(l, qlat, qpe, kall, wts):
    nb, _, t_len, _ = qlat.shape
    tq, tk = ATTN_TQ, ATTN_TK
    nq, nk = t_len // tq, t_len // tk
    rows = MLA_HEADS * tq
    kern = functools.partial(_attn_kernel, tq=tq, tk=tk)
    return pl.pallas_call(
        kern,
        grid=(nb, nq, nk),
        in_specs=[
            pl.BlockSpec((1, MLA_HEADS, tq, KV_LORA), lambda b, i, k: (b, 0, i, 0)),
            pl.BlockSpec((1, tq, 256), lambda b, i, k: (b, i, 0)),
            pl.BlockSpec((1, tk, 512),
                         lambda b, i, k: (b, jnp.minimum(k, (i * tq + tq - 1) // tk), 0)),
            _resident((1, MLA_HEADS, KV_LORA, MLA_W), lambda b, i, k: (l, 0, 0, 0)),
            pl.BlockSpec((1, 1, D_MODEL), lambda b, i, k: (l, 0, 0)),
        ],
        out_specs=pl.BlockSpec((tq, MLA_W), lambda b, i, k: (b * nq + i, 0)),
        out_shape=jax.ShapeDtypeStruct((nb * t_len, MLA_W), BF16),
        scratch_shapes=[
            pltpu.VMEM((rows, 512), BF16),
            pltpu.VMEM((rows, 1), F32),
            pltpu.VMEM((rows, 1), F32),
            pltpu.VMEM((rows, KV_LORA), F32),
        ],
        compiler_params=_params("parallel", "parallel", "arbitrary"),
        name="prompt_attn",
    )(qlat, qpe, kall, wts["w_uv_exp"], wts["norm_group"])


def _sample_attn_kernel(pt_ref, qlat_ref, qpe_ref, cnew_ref, pnew_ref, ckv_hbm, kpe_hbm, o_ref,
                        kbuf, pbuf, kb16, pb16, s_scr, sem, *, l, n_pages, page):
    b = pl.program_id(0)
    nb = pl.num_programs(0)
    slot = b % 2
    past = n_pages * page

    def page_copies(seq, sl, p):
        pg = pt_ref[seq, p]
        dst = pl.ds(pl.multiple_of(p * page, page), page)
        return (pltpu.make_async_copy(ckv_hbm.at[l, pg], kbuf.at[sl, dst, :], sem.at[0, sl]),
                pltpu.make_async_copy(kpe_hbm.at[l, pg], pbuf.at[sl, dst, :], sem.at[1, sl]))

    def issue(seq, sl):
        def body(p, c):
            for cp in page_copies(seq, sl, p):
                cp.start()
            return c
        lax.fori_loop(0, n_pages, body, 0)

    @pl.when(b == 0)
    def _():
        issue(0, 0)

    @pl.when(b + 1 < nb)
    def _():
        issue(b + 1, 1 - slot)

    def wait_body(p, c):
        for cp in page_copies(b, slot, p):
            cp.wait()
        return c
    lax.fori_loop(0, n_pages, wait_body, 0)

    q = qlat_ref[0]
    qp = qpe_ref[0]
    nrow = q.shape[0]
    for c in range(past // PAGE_CHUNK):
        sl_c = slice(c * PAGE_CHUNK, (c + 1) * PAGE_CHUNK)
        kc = kbuf[slot, sl_c, :].astype(BF16)
        pc = pbuf[slot, sl_c, :].astype(BF16)
        kb16[sl_c, :] = kc
        pb16[sl_c, :] = pc
        s_scr[:, sl_c] = (_dot_nt(q, kc) + _dot_nt(qp, pc)) * MLA_SCALE
    cn = cnew_ref[0].astype(BF16)
    pn = pnew_ref[0].astype(BF16)
    s_new = (_dot_nt(q, cn) + _dot_nt(qp, pn)) * MLA_SCALE
    t_row = lax.broadcasted_iota(jnp.int32, (nrow, NEW_PAD), 0) // MLA_HEADS
    j_col = lax.broadcasted_iota(jnp.int32, (nrow, NEW_PAD), 1)
    s_new = jnp.where(j_col <= t_row, s_new, NEG)
    s_all = s_scr[...]
    m = jnp.maximum(jnp.max(s_all, axis=-1, keepdims=True), jnp.max(s_new, axis=-1, keepdims=True))
    p_new = jnp.exp(s_new - m)
    p_all = jnp.exp(s_all - m)
    denom = jnp.sum(p_all, axis=-1, keepdims=True) + jnp.sum(p_new, axis=-1, keepdims=True)
    s_scr[...] = p_all
    acc = _dot(p_new.astype(BF16), cn)
    for c in range(past // PAGE_CHUNK):
        sl_c = slice(c * PAGE_CHUNK, (c + 1) * PAGE_CHUNK)
        acc = acc + _dot(s_scr[:, sl_c].astype(BF16), kb16[sl_c, :])
    o_ref[0] = acc / denom


def _sample_attention(l, page_table, qlat, qpe, cnew, pnew, cache_ckv, cache_kpe):
    nseq, nrow, _ = qlat.shape
    n_pages = page_table.shape[1]
    page = cache_ckv.shape[2]
    past = n_pages * page
    kern = functools.partial(_sample_attn_kernel, l=l, n_pages=n_pages, page=page)
    grid_spec = pltpu.PrefetchScalarGridSpec(
        num_scalar_prefetch=1,
        grid=(nseq,),
        in_specs=[
            pl.BlockSpec((1, nrow, KV_LORA), lambda b, pt: (b, 0, 0)),
            pl.BlockSpec((1, nrow, MLA_ROPE), lambda b, pt: (b, 0, 0)),
            pl.BlockSpec((1, NEW_PAD, KV_LORA), lambda b, pt: (b, 0, 0)),
            pl.BlockSpec((1, NEW_PAD, MLA_ROPE), lambda b, pt: (b, 0, 0)),
            pl.BlockSpec(memory_space=pl.ANY),
            pl.BlockSpec(memory_space=pl.ANY),
        ],
        out_specs=pl.BlockSpec((1, nrow, KV_LORA), lambda b, pt: (b, 0, 0)),
        scratch_shapes=[
            pltpu.VMEM((2, past, KV_LORA), F32),
            pltpu.VMEM((2, past, MLA_ROPE), F32),
            pltpu.VMEM((past, KV_LORA), BF16),
            pltpu.VMEM((past, MLA_ROPE), BF16),
            pltpu.VMEM((nrow, past), F32),
            pltpu.SemaphoreType.DMA((2, 2)),
        ],
    )
    return pl.pallas_call(
        kern,
        grid_spec=grid_spec,
        out_shape=jax.ShapeDtypeStruct((nseq, nrow, KV_LORA), F32),
        compiler_params=_params("arbitrary"),
        name="sample_attn",
    )(page_table, qlat, qpe, cnew, pnew, cache_ckv, cache_kpe)


def _uv_kernel(o_ref, w_ref, ng_ref, out_ref):
    om = _dot(o_ref[...].astype(BF16), w_ref[0])
    out_ref[...] = _rms(om, ng_ref[0][:, 0:MLA_W]).astype(BF16)


def _uv_proj(l, o_lat, wts):
    rows = o_lat.shape[0]
    return pl.pallas_call(
        _uv_kernel,
        grid=(1,),
        in_specs=[
            pl.BlockSpec((rows, MLA_HEADS * KV_LORA), lambda i: (0, 0)),
            pl.BlockSpec((1, MLA_HEADS * KV_LORA, MLA_W), lambda i: (l, 0, 0)),
            pl.BlockSpec((1, 1, D_MODEL), lambda i: (l, 0, 0)),
        ],
        out_specs=pl.BlockSpec((rows, MLA_W), lambda i: (0, 0)),
        out_shape=jax.ShapeDtypeStruct((rows, MLA_W), BF16),
        compiler_params=_params("arbitrary"),
        name="uv_proj",
    )(o_lat, wts["w_uv_flat"], wts["norm_group"])


def _sample_mem_kernel(q_ref, mk_ref, mv_ref, ng_ref, o_ref):
    ng = ng_ref[0]
    for j in range(SMEM_BLOCK):
        om = _mem_attend(q_ref[j], mk_ref[0, j], mv_ref[0, j], SUBLANES)
        o_ref[j] = _rms(om, ng[:, MLA_W + CONV_W:])


def _sample_mem(l, q, mem_k, mem_v, wts):
    nseq = q.shape[0]
    mspec = pl.BlockSpec((1, SMEM_BLOCK, MEM_TOKENS, MEM_W), lambda j: (l, j, 0, 0))
    return pl.pallas_call(
        _sample_mem_kernel,
        grid=(nseq // SMEM_BLOCK,),
        in_specs=[
            pl.BlockSpec((SMEM_BLOCK, SUBLANES, MEM_W), lambda j: (j, 0, 0)),
            mspec, mspec,
            pl.BlockSpec((1, 1, D_MODEL), lambda j: (l, 0, 0)),
        ],
        out_specs=pl.BlockSpec((SMEM_BLOCK, SUBLANES, MEM_W), lambda j: (j, 0, 0)),
        out_shape=jax.ShapeDtypeStruct((nseq, SUBLANES, MEM_W), F32),
        compiler_params=_params("parallel"),
        name="sample_mem",
    )(q, mem_k, mem_v, wts["norm_group"])


def _ffn_kernel(*refs, tm, s, has_carry, widths):
    it = iter(refs)
    x_ref = next(it)
    piece_refs = [next(it) for _ in widths]
    (wout_ref, npost_ref, npre_ref, wg_ref, wu_ref, fcw_ref, fcb_ref, wd_ref,
     npf_ref) = [next(it) for _ in range(9)]
    c0_ref = next(it) if has_carry else None
    y_ref, fnew_ref, gbuf = next(it), next(it), next(it)

    i = pl.program_id(1)
    base = _carry_base(s)

    att = None
    off = 0
    for pref, wd in zip(piece_refs, widths):
        part = _dot(pref[...], wout_ref[0, off:off + wd, :])
        att = part if att is None else att + part
        off += wd
    x1 = x_ref[...] + _rms(att, npost_ref[0])
    h2 = _rms(x1, npre_ref[0]).astype(BF16)
    g = _dot(h2, wg_ref[0])
    up = _dot(h2, wu_ref[0])

    @pl.when(i == 0)
    def _():
        if has_carry:
            gbuf[base - 2 * s:base, :] = c0_ref[0]
        else:
            gbuf[0:base, :] = jnp.zeros((base, D_FF), F32)

    gbuf[base:base + tm, :] = g
    w = fcw_ref[0]
    gc = (w[0:1] * gbuf[base - 2 * s:base - 2 * s + tm, :]
          + w[1:2] * gbuf[base - s:base - s + tm, :] + w[2:3] * g + fcb_ref[0])
    newc = gbuf[base + tm - 2 * s:base + tm, :]
    gbuf[base - 2 * s:base, :] = newc
    fnew_ref[0] = newc
    a = (gc * (1.0 / (1.0 + jnp.exp(-gc))) * up).astype(BF16)
    y_ref[...] = x1 + _rms(_dot(a, wd_ref[0]), npf_ref[0])


def _out_ffn(l, x, pieces, nb, t_len, tm, s, wts, carry0=None):
    nt = t_len // tm
    rows = nb * t_len
    has_carry = carry0 is not None
    widths = tuple(p.shape[1] for p in pieces)
    row_map = lambda b, i: (b * nt + i, 0)
    lay = lambda b, i: (l, 0, 0)
    in_specs = [pl.BlockSpec((tm, D_MODEL), row_map)]
    in_specs += [pl.BlockSpec((tm, wd), row_map) for wd in widths]
    in_specs += [
        _resident((1, D_MODEL, D_MODEL), lay),
        pl.BlockSpec((1, 1, D_MODEL), lay),
        pl.BlockSpec((1, 1, D_MODEL), lay),
        _resident((1, D_MODEL, D_FF), lay),
        _resident((1, D_MODEL, D_FF), lay),
        pl.BlockSpec((1, 3, D_FF), lay),
        pl.BlockSpec((1, 1, D_FF), lay),
        _resident((1, D_FF, D_MODEL), lay),
        pl.BlockSpec((1, 1, D_MODEL), lay),
    ]
    args = [x, *pieces, wts["w_out"], wts["norm_post_mix"], wts["norm_pre_ffn"], wts["w_gate"],
            wts["w_up"], wts["ffn_conv_w"], wts["ffn_conv_b"], wts["w_down"],
            wts["norm_post_ffn"]]
    if has_carry:
        in_specs.append(pl.BlockSpec((1, 2 * s, D_FF), lambda b, i: (b, 0, 0)))
        args.append(carry0)
    kern = functools.partial(_ffn_kernel, tm=tm, s=s, has_carry=has_carry, widths=widths)
    return pl.pallas_call(
        kern,
        grid=(nb, nt),
        in_specs=in_specs,
        out_specs=[pl.BlockSpec((tm, D_MODEL), row_map),
                   pl.BlockSpec((1, 2 * s, D_FF), lambda b, i: (b, 0, 0))],
        out_shape=[jax.ShapeDtypeStruct((rows, D_MODEL), F32),
                   jax.ShapeDtypeStruct((nb, 2 * s, D_FF), F32)],
        scratch_shapes=[pltpu.VMEM((_carry_base(s) + tm, D_FF), F32)],
        compiler_params=_params("parallel", "arbitrary"),
        name="out_ffn",
    )(*args)


def _rot_pair(w):
    half = MLA_ROPE // 2
    return jnp.concatenate([-w[..., half:], w[..., :half]], axis=-1)


def _prep_weights(norm_pre_mix, w_in, norm_q, w_uq, norm_kv, w_uk, w_uv, conv_w, norm_mem,
                  w_mem_k, w_mem_v, norm_group, w_out, norm_post_mix, norm_pre_ffn, w_gate, w_up,
                  ffn_conv_w, ffn_conv_b, w_down, norm_post_ffn):
    o0 = Q_LORA
    o1 = o0 + KV_LORA
    o2 = o1 + MLA_ROPE
    o3 = o2 + CONV_W
    o4 = o3 + CONV_W
    o5 = o4 + CONV_W
    kpe_w = w_in[..., o1:o2]
    w_in_r = jnp.concatenate(
        [w_in[..., :o1], w_in[..., o2:], kpe_w, _rot_pair(kpe_w),
         jnp.zeros(w_in.shape[:2] + (128 - 2 * MLA_ROPE,), w_in.dtype)], axis=-1).astype(BF16)

    per_head = w_uq.reshape(DEPTH, Q_LORA, MLA_HEADS, MLA_NOPE + MLA_ROPE)
    nope = per_head[..., :MLA_NOPE].reshape(DEPTH, Q_LORA, MLA_HEADS * MLA_NOPE)
    pe = per_head[..., MLA_NOPE:]
    w_uq_r = jnp.concatenate(
        [nope, pe.reshape(DEPTH, Q_LORA, -1), _rot_pair(pe).reshape(DEPTH, Q_LORA, -1)],
        axis=-1).astype(BF16)

    wt = jnp.transpose(w_uk, (0, 2, 3, 1)).reshape(DEPTH, MLA_HEADS // 2, 2, MLA_NOPE, KV_LORA)
    w_uk2 = jnp.einsum("ljanr,ab->ljanbr", wt, jnp.eye(2, dtype=wt.dtype)).reshape(
        DEPTH, MLA_HEADS // 2, 2 * MLA_NOPE, 2 * KV_LORA).astype(BF16)

    w_uv_exp = jnp.einsum("lrhv,hg->lhrgv", w_uv, jnp.eye(MLA_HEADS, dtype=w_uv.dtype)).reshape(
        DEPTH, MLA_HEADS, KV_LORA, MLA_W).astype(BF16)

    vec = lambda a: a.reshape(DEPTH, 1, a.shape[-1])
    return dict(
        norm_pre_mix=vec(norm_pre_mix), w_in=w_in_r, norm_q=vec(norm_q), w_uq=w_uq_r,
        norm_kv=vec(norm_kv), w_uk2=w_uk2, w_uv_exp=w_uv_exp,
        w_uv_flat=w_uv_exp.reshape(DEPTH, MLA_HEADS * KV_LORA, MLA_W), conv_w=conv_w,
        norm_mem=vec(norm_mem), w_mem_k=w_mem_k.astype(BF16), w_mem_v=w_mem_v.astype(BF16),
        norm_group=vec(norm_group), w_out=w_out.astype(BF16), norm_post_mix=vec(norm_post_mix),
        norm_pre_ffn=vec(norm_pre_ffn), w_gate=w_gate.astype(BF16), w_up=w_up.astype(BF16),
        ffn_conv_w=ffn_conv_w, ffn_conv_b=vec(ffn_conv_b), w_down=w_down.astype(BF16),
        norm_post_ffn=vec(norm_post_ffn))


def _rope_tables(pos):
    half = MLA_ROPE // 2
    freqs = ROPE_THETA ** (-jnp.arange(half, dtype=F32) * (2.0 / MLA_ROPE))
    ang = pos.astype(F32)[:, None] * freqs[None, :]
    cos = jnp.cos(ang)
    sin = jnp.sin(ang)
    cos32 = jnp.concatenate([cos, cos], axis=-1)
    sin32 = jnp.concatenate([sin, sin], axis=-1)
    tabk = jnp.concatenate([cos32, sin32, jnp.zeros((pos.shape[0], 64), F32)], axis=-1)
    return jnp.tile(cos32, (1, MLA_HEADS)), jnp.tile(sin32, (1, MLA_HEADS)), tabk


def _to_time_major(a):
    return jnp.transpose(a, (1, 0, 2)).reshape(-1, a.shape[-1])


def _from_time_major(a, t_len):
    return jnp.transpose(a.reshape(t_len, -1, a.shape[-1]), (1, 0, 2))


def kernel(x_prompt, x_sample, mem_prompt, cache_ckv, cache_kpe, cache_mem_k, cache_mem_v, state_conv, state_ffn_conv, page_table, norm_pre_mix, w_in, norm_q, w_uq, norm_kv, w_uk, w_uv, conv_w, norm_mem, w_mem_k, w_mem_v, norm_group, w_out, norm_post_mix, norm_pre_ffn, w_gate, w_up, ffn_conv_w, ffn_conv_b, w_down, norm_post_ffn):
    nb, seq, _ = x_prompt.shape
    nseq, dseq, _ = x_sample.shape
    n_pages = page_table.shape[1]
    page = cache_ckv.shape[2]
    past = n_pages * page
    wts = _prep_weights(norm_pre_mix, w_in, norm_q, w_uq, norm_kv, w_uk, w_uv, conv_w, norm_mem,
                        w_mem_k, w_mem_v, norm_group, w_out, norm_post_mix, norm_pre_ffn, w_gate,
                        w_up, ffn_conv_w, ffn_conv_b, w_down, norm_post_ffn)

    tabs_p = _rope_tables(jnp.arange(seq, dtype=jnp.int32))
    pos_s = jnp.repeat(past + jnp.arange(dseq, dtype=jnp.int32), nseq)
    tabs_s = _rope_tables(pos_s)

    mem_k_all, mem_v_all = _mem_kv(mem_prompt, wts["norm_mem"], wts["w_mem_k"], wts["w_mem_v"])
    smem_k = cache_mem_k.reshape(DEPTH, nseq, MEM_TOKENS, MEM_W)
    smem_v = cache_mem_v.reshape(DEPTH, nseq, MEM_TOKENS, MEM_W)

    yp = x_prompt.reshape(nb * seq, D_MODEL)
    ys = _to_time_major(x_sample)
    srows = nseq * dseq
    p_ckv, p_kpe, p_conv, p_ffn = [], [], [], []
    s_ckv, s_kpe, s_conv, s_ffn = [], [], [], []
    for l in range(DEPTH):
        qlat, qpe, kall, ckv, kpe, mbc, cnew = _in_proj(
            l, yp, nb, seq, PROMPT_TM, 1, tabs_p, wts,
            mem_kv=(mem_k_all, mem_v_all))
        ma = _prompt_attention(l, qlat, qpe, kall, wts)
        yp, fnew = _out_ffn(l, yp, [ma, mbc], nb, seq, PROMPT_TM, 1, wts)
        p_ckv.append(ckv.reshape(nb, seq, KV_LORA))
        p_kpe.append(kpe.reshape(nb, seq, MLA_ROPE))
        p_conv.append(cnew)
        p_ffn.append(fnew)

        c0 = _to_time_major(state_conv[l])[None]
        f0 = _to_time_major(state_ffn_conv[l])[None]
        qlat, qpe, _, ckv, kpe, mb, qm, cnew = _in_proj(
            l, ys, 1, srows, srows, nseq, tabs_s, wts, carry0=c0)
        ckv_bt = _from_time_major(ckv, dseq)
        kpe_bt = _from_time_major(kpe, dseq)
        qlat_bt = jnp.transpose(qlat[0].reshape(MLA_HEADS, dseq, nseq, KV_LORA),
                                (2, 1, 0, 3)).reshape(nseq, dseq * MLA_HEADS, KV_LORA)
        qpe_bt = jnp.transpose(qpe[0].reshape(dseq, nseq, MLA_HEADS, MLA_ROPE),
                               (1, 0, 2, 3)).reshape(nseq, dseq * MLA_HEADS, MLA_ROPE)
        padn = ((0, 0), (0, NEW_PAD - dseq), (0, 0))
        o_lat = _sample_attention(l, page_table, qlat_bt, qpe_bt, jnp.pad(ckv_bt, padn),
                                  jnp.pad(kpe_bt, padn), cache_ckv, cache_kpe)
        o_lat = _to_time_major(o_lat.reshape(nseq, dseq, MLA_HEADS * KV_LORA))
        ma = _uv_proj(l, o_lat, wts)
        qm_bt = jnp.pad(_from_time_major(qm, dseq), ((0, 0), (0, SUBLANES - dseq), (0, 0)))
        mc = _sample_mem(l, qm_bt, smem_k, smem_v, wts)
        mc = _to_time_major(mc[:, :dseq]).astype(BF16)
        ys, fnew = _out_ffn(l, ys, [ma, mb, mc], 1, srows, srows, nseq, wts, carry0=f0)
        s_ckv.append(ckv_bt)
        s_kpe.append(kpe_bt)
        s_conv.append(_from_time_major(cnew[0], 2))
        s_ffn.append(_from_time_major(fnew[0], 2))

    mem_shape = (DEPTH, nb, MEM_TOKENS, MEM_HEADS, HEAD_DIM)
    return (yp.reshape(nb, seq, D_MODEL), _from_time_major(ys, dseq),
            jnp.stack(p_ckv), jnp.stack(p_kpe),
            mem_k_all.reshape(mem_shape), mem_v_all.reshape(mem_shape),
            jnp.stack(p_conv), jnp.stack(p_ffn),
            jnp.stack(s_ckv), jnp.stack(s_kpe), jnp.stack(s_conv), jnp.stack(s_ffn))
```

```python
import functools

import jax
import jax.numpy as jnp
import numpy as np
from jax import lax
from jax.experimental import pallas as pl
from jax.experimental.pallas import tpu as pltpu

F32 = jnp.float32
BF16 = jnp.bfloat16

D_MODEL = 1024
DEPTH = 4
HEAD_DIM = 64
MLA_HEADS = 8
MLA_NOPE = 64
MLA_ROPE = 32
MLA_V = 64
MLA_W = MLA_HEADS * MLA_V
KV_LORA = 256
Q_LORA = 384
CONV_W = 256
MEM_HEADS = 4
MEM_W = MEM_HEADS * HEAD_DIM
MEM_TOKENS = 256
D_FF = 2816
ROPE_THETA = 10000.0
EPS = 1e-6
MLA_SCALE = (MLA_NOPE + MLA_ROPE) ** -0.5
MEM_SCALE = HEAD_DIM ** -0.5
NEG = -1e30
LOG2E = 1.4426950408889634

Z_CQ = 0
Z_CKV = Z_CQ + Q_LORA
Z_GB = Z_CKV + KV_LORA
Z_GC = Z_GB + CONV_W
Z_HV = Z_GC + CONV_W
Z_QM = Z_HV + CONV_W
Z_KPE = Z_QM + MEM_W
Z_COLS = Z_KPE + 128

SUBLANES = 8
VMEM_LIMIT_BYTES = 56 * 1024 * 1024

PROMPT_TM = 256
ATTN_TQ = 256
ATTN_TK = 512
PAGE_CHUNK = 2048
NEW_PAD = 16
SMEM_BLOCK = 8
DMA_UNROLL = 8


def _rms(x, g):
    return x * lax.rsqrt(jnp.mean(x * x, axis=-1, keepdims=True) + EPS) * g


def _dot(a, b):
    return jnp.dot(a, b, preferred_element_type=F32)


def _dot_nt(a, b):
    return lax.dot_general(a, b, (((1,), (1,)), ((), ())), preferred_element_type=F32)


def _resident(shape, index_map):
    return pl.BlockSpec(shape, index_map, pipeline_mode=pl.Buffered(1))


def _params(*sem):
    return pltpu.CompilerParams(dimension_semantics=sem, vmem_limit_bytes=VMEM_LIMIT_BYTES)


def _head_mask(lane, h, width):
    return (lane >= width * h) & (lane < width * (h + 1))


def _carry_base(s):
    return -(-2 * s // SUBLANES) * SUBLANES


def _mem_kv_kernel(mem_ref, g_ref, wk_ref, wv_ref, k_ref, v_ref):
    m = _rms(mem_ref[0], g_ref[0]).astype(BF16)
    k_ref[0, 0] = _dot_nt(wk_ref[0], m)
    v_ref[0, 0] = _dot_nt(wv_ref[0], m)


def _mem_kv(mem, norm_mem, wk, wv):
    nb = mem.shape[0]
    out = jax.ShapeDtypeStruct((DEPTH, nb, MEM_W, MEM_TOKENS), F32)
    return pl.pallas_call(
        _mem_kv_kernel,
        grid=(DEPTH, nb),
        in_specs=[
            pl.BlockSpec((1, MEM_TOKENS, D_MODEL), lambda l, b: (b, 0, 0)),
            pl.BlockSpec((1, 1, D_MODEL), lambda l, b: (l, 0, 0)),
            pl.BlockSpec((1, MEM_W, D_MODEL), lambda l, b: (l, 0, 0)),
            pl.BlockSpec((1, MEM_W, D_MODEL), lambda l, b: (l, 0, 0)),
        ],
        out_specs=[
            pl.BlockSpec((1, 1, MEM_W, MEM_TOKENS), lambda l, b: (l, b, 0, 0)),
            pl.BlockSpec((1, 1, MEM_W, MEM_TOKENS), lambda l, b: (l, b, 0, 0)),
        ],
        out_shape=[out, out],
        compiler_params=_params("parallel", "parallel"),
        name="mem_kv",
    )(mem, norm_mem, wk, wv)


def _mem_attend(qm, mk_t, mv_t, rows):
    lane = lax.broadcasted_iota(jnp.int32, (rows, MEM_W), 1)
    masks = [_head_mask(lane, h, HEAD_DIM) for h in range(MEM_HEADS)]
    qs = jnp.concatenate([jnp.where(m, qm, 0.0) for m in masks], axis=0).astype(BF16)
    sc = _dot(qs, mk_t.astype(BF16)) * MEM_SCALE
    p = jnp.exp(sc - jnp.max(sc, axis=-1, keepdims=True))
    o = _dot_nt(p.astype(BF16), mv_t.astype(BF16)) / jnp.sum(p, axis=-1, keepdims=True)
    om = jnp.where(masks[0], o[0:rows], 0.0)
    for h in range(1, MEM_HEADS):
        om = om + jnp.where(masks[h], o[h * rows:(h + 1) * rows], 0.0)
    return om


def _in_proj_kernel(*refs, tm, s, has_carry, do_mem):
    it = iter(refs)
    (x_ref, cosq_ref, sinq_ref, tabk_ref, npm_ref, win_ref, nq_ref, wuq_ref, nkv_ref,
     wuk_ref, cw_ref, ng_ref) = [next(it) for _ in range(12)]
    c0_ref = next(it) if has_carry else None
    if do_mem:
        mk_ref, mv_ref = next(it), next(it)
    qlat_ref, qpe_ref, kall_ref, ckvt_ref, ckv_ref, kpe_ref = [next(it) for _ in range(6)]
    if do_mem:
        mbc_ref = next(it)
    else:
        mb_ref, qm_ref = next(it), next(it)
    cnew_ref = next(it)
    ubuf = next(it)

    i = pl.program_id(1)
    base = _carry_base(s)

    h = _rms(x_ref[...], npm_ref[0]).astype(BF16)
    z = _dot(h, win_ref[0])

    cqn = _rms(z[:, Z_CQ:Z_CKV], nq_ref[0]).astype(BF16)
    q = _dot(cqn, wuq_ref[0])
    qpe = q[:, 512:768] * cosq_ref[...] + q[:, 768:1024] * sinq_ref[...]
    qpe_ref[0] = qpe.astype(BF16)
    for j in range(MLA_HEADS // 2):
        ql = _dot(q[:, 128 * j:128 * (j + 1)].astype(BF16), wuk_ref[0, j])
        qlat_ref[0, 2 * j] = ql[:, :KV_LORA].astype(BF16)
        qlat_ref[0, 2 * j + 1] = ql[:, KV_LORA:].astype(BF16)

    ckv = _rms(z[:, Z_CKV:Z_GB], nkv_ref[0])
    ckv_ref[...] = ckv
    ckvt_ref[0] = ckv.T.astype(BF16)
    t = z[:, Z_KPE:Z_COLS] * tabk_ref[...]
    r = t + pltpu.roll(t, 96, 1)
    lane = lax.broadcasted_iota(jnp.int32, (tm, 128), 1)
    kp = jnp.where(lane < MLA_ROPE, r, 0.0)
    kpe_ref[0] = kp.T[0:MLA_ROPE, :]
    kp = kp + pltpu.roll(kp, 32, 1)
    kp = (kp + pltpu.roll(kp, 64, 1)).astype(BF16)
    kall_ref[0, :, 0:KV_LORA] = ckv.astype(BF16)
    kall_ref[0, :, KV_LORA:KV_LORA + 128] = kp
    kall_ref[0, :, KV_LORA + 128:KV_LORA + 256] = kp

    u = z[:, Z_GC:Z_HV] * z[:, Z_HV:Z_QM]

    @pl.when(i == 0)
    def _():
        if has_carry:
            ubuf[base - 2 * s:base, :] = c0_ref[0]
        else:
            ubuf[0:base, :] = jnp.zeros((base, CONV_W), F32)

    ubuf[base:base + tm, :] = u
    w = cw_ref[0]
    cv = (w[0:1] * ubuf[base - 2 * s:base - 2 * s + tm, :]
          + w[1:2] * ubuf[base - s:base - s + tm, :] + w[2:3] * u)
    newc = ubuf[base + tm - 2 * s:base + tm, :]
    ubuf[base - 2 * s:base, :] = newc
    cnew_ref[0] = newc
    ng = ng_ref[0]
    mb = _rms(z[:, Z_GB:Z_GC] * cv, ng[:, MLA_W:MLA_W + CONV_W]).astype(BF16)

    qm = z[:, Z_QM:Z_KPE]
    if do_mem:
        om = _mem_attend(qm, mk_ref[0, 0], mv_ref[0, 0], tm)
        mbc_ref[:, 0:CONV_W] = mb
        mbc_ref[:, CONV_W:] = _rms(om, ng[:, MLA_W + CONV_W:]).astype(BF16)
    else:
        mb_ref[...] = mb
        qm_ref[...] = qm


def _in_proj(l, x, nb, t_len, tm, s, tabs, wts, carry0=None, mem_kv=None):
    nt = t_len // tm
    rows = nb * t_len
    has_carry = carry0 is not None
    do_mem = mem_kv is not None
    row_map = lambda b, i: (b * nt + i, 0)
    tab_map = lambda b, i: (i, 0)
    lay = lambda b, i: (l, 0, 0)
    in_specs = [
        pl.BlockSpec((tm, D_MODEL), row_map),
        pl.BlockSpec((tm, 256), tab_map),
        pl.BlockSpec((tm, 256), tab_map),
        pl.BlockSpec((tm, 128), tab_map),
        pl.BlockSpec((1, 1, D_MODEL), lay),
        _resident((1, D_MODEL, Z_COLS), lay),
        pl.BlockSpec((1, 1, Q_LORA), lay),
        _resident((1, Q_LORA, 1024), lay),
        pl.BlockSpec((1, 1, KV_LORA), lay),
        _resident((1, MLA_HEADS // 2, 128, 2 * KV_LORA), lambda b, i: (l, 0, 0, 0)),
        pl.BlockSpec((1, 3, CONV_W), lay),
        pl.BlockSpec((1, 1, D_MODEL), lay),
    ]
    args = [x, *tabs, wts["norm_pre_mix"], wts["w_in"], wts["norm_q"], wts["w_uq"],
            wts["norm_kv"], wts["w_uk2"], wts["conv_w"], wts["norm_group"]]
    if has_carry:
        in_specs.append(pl.BlockSpec((1, 2 * s, CONV_W), lambda b, i: (b, 0, 0)))
        args.append(carry0)
    if do_mem:
        mspec = pl.BlockSpec((1, 1, MEM_W, MEM_TOKENS), lambda b, i: (l, b, 0, 0))
        in_specs += [mspec, mspec]
        args += list(mem_kv)
    out_shape = [
        jax.ShapeDtypeStruct((nb, MLA_HEADS, t_len, KV_LORA), BF16),
        jax.ShapeDtypeStruct((nb, t_len, 256), BF16),
        jax.ShapeDtypeStruct((nb, t_len, 512), BF16),
        jax.ShapeDtypeStruct((nb, KV_LORA, t_len), BF16),
        jax.ShapeDtypeStruct((rows, KV_LORA), F32),
        jax.ShapeDtypeStruct((nb, MLA_ROPE, t_len), F32),
    ]
    out_specs = [
        pl.BlockSpec((1, MLA_HEADS, tm, KV_LORA), lambda b, i: (b, 0, i, 0)),
        pl.BlockSpec((1, tm, 256), lambda b, i: (b, i, 0)),
        pl.BlockSpec((1, tm, 512), lambda b, i: (b, i, 0)),
        pl.BlockSpec((1, KV_LORA, tm), lambda b, i: (b, 0, i)),
        pl.BlockSpec((tm, KV_LORA), row_map),
        pl.BlockSpec((1, MLA_ROPE, tm), lambda b, i: (b, 0, i)),
    ]
    if do_mem:
        out_shape.append(jax.ShapeDtypeStruct((rows, 512), BF16))
        out_specs.append(pl.BlockSpec((tm, 512), row_map))
    else:
        out_shape += [jax.ShapeDtypeStruct((rows, CONV_W), BF16),
                      jax.ShapeDtypeStruct((rows, MEM_W), F32)]
        out_specs += [pl.BlockSpec((tm, CONV_W), row_map), pl.BlockSpec((tm, MEM_W), row_map)]
    out_shape.append(jax.ShapeDtypeStruct((nb, 2 * s, CONV_W), F32))
    out_specs.append(pl.BlockSpec((1, 2 * s, CONV_W), lambda b, i: (b, 0, 0)))
    kern = functools.partial(_in_proj_kernel, tm=tm, s=s, has_carry=has_carry, do_mem=do_mem)
    return pl.pallas_call(
        kern,
        grid=(nb, nt),
        in_specs=in_specs,
        out_specs=out_specs,
        out_shape=out_shape,
        scratch_shapes=[pltpu.VMEM((_carry_base(s) + tm, CONV_W), F32)],
        compiler_params=_params("parallel", "arbitrary"),
        name="in_proj_mem" if do_mem else "in_proj",
    )(*args)


def _causal_pairs(nq, tq, tk):
    it, kt = [], []
    for i in range(nq):
        for k in range((i * tq + tq - 1) // tk + 1):
            it.append(i)
            kt.append(k)
    return np.asarray(it, np.int32), np.asarray(kt, np.int32)


def _attn_kernel(it_ref, kt_ref, qlat_ref, qpe_ref, kall_ref, ckvt_ref, wuvt_ref, ngb_ref, o_ref,
                 q2, m_scr, l_scr, acct, *, tq, tk):
    j = pl.program_id(1)
    i = it_ref[j]
    k = kt_ref[j]
    last_k = (i * tq + tq - 1) // tk
    cols = MLA_HEADS * tq
    c_exp = MLA_SCALE * LOG2E

    @pl.when(k == 0)
    def _():
        lane = lax.broadcasted_iota(jnp.int32, (tq, 256), 1)
        qp = qpe_ref[0]
        zero = jnp.zeros_like(qp)
        for h in range(MLA_HEADS):
            q2[h * tq:(h + 1) * tq, 0:KV_LORA] = qlat_ref[0, h]
            q2[h * tq:(h + 1) * tq, KV_LORA:] = jnp.where(_head_mask(lane, h, MLA_ROPE), qp, zero)
        m_scr[...] = jnp.full((1, cols), NEG, F32)
        l_scr[...] = jnp.zeros((1, cols), F32)
        acct[...] = jnp.zeros((KV_LORA, cols), F32)

    def scores():
        return _dot_nt(kall_ref[0], q2[...])

    def step(sc):
        m_prev = m_scr[...]
        m_new = jnp.maximum(m_prev, jnp.max(sc, axis=0, keepdims=True))
        alpha = jnp.exp2((m_prev - m_new) * c_exp)
        p = jnp.exp2((sc - m_new) * c_exp)
        l_scr[...] = alpha * l_scr[...] + jnp.sum(p, axis=0, keepdims=True)
        acct[...] = alpha * acct[...] + _dot(ckvt_ref[0], p.astype(BF16))
        m_scr[...] = m_new

    @pl.when(k < last_k)
    def _():
        step(scores())

    @pl.when(k == last_k)
    def _():
        k_pos = k * tk + lax.broadcasted_iota(jnp.int32, (tk, cols), 0)
        q_pos = i * tq + (lax.broadcasted_iota(jnp.int32, (tk, cols), 1) & (tq - 1))
        step(jnp.where(k_pos <= q_pos, scores(), NEG))
        ot = (acct[...] * (1.0 / l_scr[...])).astype(BF16)
        omt = jnp.concatenate(
            [_dot(wuvt_ref[0, h], ot[:, h * tq:(h + 1) * tq]) for h in range(MLA_HEADS)], axis=0)
        y = omt * lax.rsqrt(jnp.mean(omt * omt, axis=0, keepdims=True) + EPS) * ngb_ref[0]
        o_ref[...] = y.T.astype(BF16)


def _prompt_attention(l, qlat, qpe, kall, ckvt, wts):
    nb, _, t_len, _ = qlat.shape
    tq, tk = ATTN_TQ, ATTN_TK
    nq = t_len // tq
    cols = MLA_HEADS * tq
    it, kt = _causal_pairs(nq, tq, tk)
    kern = functools.partial(_attn_kernel, tq=tq, tk=tk)
    grid_spec = pltpu.PrefetchScalarGridSpec(
        num_scalar_prefetch=2,
        grid=(nb, it.shape[0]),
        in_specs=[
            pl.BlockSpec((1, MLA_HEADS, tq, KV_LORA), lambda b, j, it, kt: (b, 0, it[j], 0)),
            pl.BlockSpec((1, tq, 256), lambda b, j, it, kt: (b, it[j], 0)),
            pl.BlockSpec((1, tk, 512), lambda b, j, it, kt: (b, kt[j], 0)),
            pl.BlockSpec((1, KV_LORA, tk), lambda b, j, it, kt: (b, 0, kt[j])),
            _resident((1, MLA_HEADS, MLA_V, KV_LORA), lambda b, j, it, kt: (l, 0, 0, 0)),
            pl.BlockSpec((1, MLA_W, tq), lambda b, j, it, kt: (l, 0, 0)),
        ],
        out_specs=pl.BlockSpec((tq, MLA_W), lambda b, j, it, kt: (b * nq + it[j], 0)),
        scratch_shapes=[
            pltpu.VMEM((cols, 512), BF16),
            pltpu.VMEM((1, cols), F32),
            pltpu.VMEM((1, cols), F32),
            pltpu.VMEM((KV_LORA, cols), F32),
        ],
    )
    return pl.pallas_call(
        kern,
        grid_spec=grid_spec,
        out_shape=jax.ShapeDtypeStruct((nb * t_len, MLA_W), BF16),
        compiler_params=_params("parallel", "arbitrary"),
        name="prompt_attn",
    )(jnp.asarray(it), jnp.asarray(kt), qlat, qpe, kall, ckvt, wts["w_uv_t"], wts["ng_attn"])


def _sample_attn_kernel(pt_ref, qlat_ref, qpe_ref, cnew_ref, pnew_ref, ckv_hbm, kpe_hbm, o_ref,
                        kbuf, pbuf, kb16, s_scr, sem, *, l, n_pages, page):
    b = pl.program_id(0)
    nb = pl.num_programs(0)
    slot = b % 2
    past = n_pages * page

    def page_copies(seq, sl, p):
        pg = pt_ref[seq, p]
        dst = pl.ds(pl.multiple_of(p * page, page), page)
        return (pltpu.make_async_copy(ckv_hbm.at[l, pg], kbuf.at[sl, dst, :], sem.at[0, sl]),
                pltpu.make_async_copy(kpe_hbm.at[l, pg], pbuf.at[sl, :, dst], sem.at[1, sl]))

    def issue(seq, sl):
        def body(p, c):
            for cp in page_copies(seq, sl, p):
                cp.start()
            return c
        lax.fori_loop(0, n_pages, body, 0, unroll=DMA_UNROLL)

    @pl.when(b == 0)
    def _():
        issue(0, 0)

    @pl.when(b + 1 < nb)
    def _():
        issue(b + 1, 1 - slot)

    def wait_body(p, c):
        for cp in page_copies(b, slot, p):
            cp.wait()
        return c
    lax.fori_loop(0, n_pages, wait_body, 0, unroll=DMA_UNROLL)

    q = qlat_ref[0]
    qp = qpe_ref[0]
    nrow = q.shape[0]
    nch = past // PAGE_CHUNK
    for c in range(nch):
        sl_c = slice(c * PAGE_CHUNK, (c + 1) * PAGE_CHUNK)
        kb16[sl_c, :] = kbuf[slot, sl_c, :].astype(BF16)
    s_scr[...] = (_dot_nt(q, kb16[...]) + _dot(qp, pbuf[slot].astype(BF16))) * MLA_SCALE

    @pl.when(b >= 0)
    def _():
        cn = cnew_ref[0].astype(BF16)
        pn = pnew_ref[0].astype(BF16)
        s_new = (_dot_nt(q, cn) + _dot(qp, pn)) * MLA_SCALE
        t_row = lax.broadcasted_iota(jnp.int32, (nrow, NEW_PAD), 0) // MLA_HEADS
        j_col = lax.broadcasted_iota(jnp.int32, (nrow, NEW_PAD), 1)
        s_new = jnp.where(j_col <= t_row, s_new, NEG)
        s_all = s_scr[...]
        m = jnp.maximum(jnp.max(s_all, axis=-1, keepdims=True),
                        jnp.max(s_new, axis=-1, keepdims=True))
        p_new = jnp.exp(s_new - m)
        p_all = jnp.exp(s_all - m)
        denom = jnp.sum(p_all, axis=-1, keepdims=True) + jnp.sum(p_new, axis=-1, keepdims=True)
        s_scr[...] = p_all
        accs = [_dot(p_new.astype(BF16), cn), None]
        for c in range(nch):
            sl_c = slice(c * PAGE_CHUNK, (c + 1) * PAGE_CHUNK)
            d = _dot(s_scr[:, sl_c].astype(BF16), kb16[sl_c, :])
            accs[c % 2] = d if accs[c % 2] is None else accs[c % 2] + d
        acc = accs[0] if accs[1] is None else accs[0] + accs[1]
        o_ref[0] = acc / denom


def _sample_attention(l, page_table, qlat, qpe, cnew, pnew_t, cache_ckv, cache_kpe_t):
    nseq, nrow, _ = qlat.shape
    n_pages = page_table.shape[1]
    page = cache_ckv.shape[2]
    past = n_pages * page
    kern = functools.partial(_sample_attn_kernel, l=l, n_pages=n_pages, page=page)
    grid_spec = pltpu.PrefetchScalarGridSpec(
        num_scalar_prefetch=1,
        grid=(nseq,),
        in_specs=[
            pl.BlockSpec((1, nrow, KV_LORA), lambda b, pt: (b, 0, 0)),
            pl.BlockSpec((1, nrow, MLA_ROPE), lambda b, pt: (b, 0, 0)),
            pl.BlockSpec((1, NEW_PAD, KV_LORA), lambda b, pt: (b, 0, 0)),
            pl.BlockSpec((1, MLA_ROPE, NEW_PAD), lambda b, pt: (b, 0, 0)),
            pl.BlockSpec(memory_space=pl.ANY),
            pl.BlockSpec(memory_space=pl.ANY),
        ],
        out_specs=pl.BlockSpec((1, nrow, KV_LORA), lambda b, pt: (b, 0, 0)),
        scratch_shapes=[
            pltpu.VMEM((2, past, KV_LORA), F32),
            pltpu.VMEM((2, MLA_ROPE, past), F32),
            pltpu.VMEM((past, KV_LORA), BF16),
            pltpu.VMEM((nrow, past), F32),
            pltpu.SemaphoreType.DMA((2, 2)),
        ],
    )
    return pl.pallas_call(
        kern,
        grid_spec=grid_spec,
        out_shape=jax.ShapeDtypeStruct((nseq, nrow, KV_LORA), F32),
        compiler_params=_params("arbitrary"),
        name="sample_attn",
    )(page_table, qlat, qpe, cnew, pnew_t, cache_ckv, cache_kpe_t)


def _uv_kernel(o_ref, w_ref, ng_ref, out_ref):
    om = _dot(o_ref[...].astype(BF16), w_ref[0])
    out_ref[...] = _rms(om, ng_ref[0][:, 0:MLA_W]).astype(BF16)


def _uv_proj(l, o_lat, wts):
    rows = o_lat.shape[0]
    return pl.pallas_call(
        _uv_kernel,
        grid=(1,),
        in_specs=[
            pl.BlockSpec((rows, MLA_HEADS * KV_LORA), lambda i: (0, 0)),
            pl.BlockSpec((1, MLA_HEADS * KV_LORA, MLA_W), lambda i: (l, 0, 0)),
            pl.BlockSpec((1, 1, D_MODEL), lambda i: (l, 0, 0)),
        ],
        out_specs=pl.BlockSpec((rows, MLA_W), lambda i: (0, 0)),
        out_shape=jax.ShapeDtypeStruct((rows, MLA_W), BF16),
        compiler_params=_params("arbitrary"),
        name="uv_proj",
    )(o_lat, wts["w_uv_flat"], wts["norm_group"])


def _sample_mem_kernel(q_ref, mk_ref, mv_ref, ng_ref, o_ref):
    ng = ng_ref[0]
    for j in range(SMEM_BLOCK):
        om = _mem_attend(q_ref[j], mk_ref[0, j], mv_ref[0, j], SUBLANES)
        o_ref[j] = _rms(om, ng[:, MLA_W + CONV_W:])


def _sample_mem(l, q, mem_k, mem_v, wts):
    nseq = q.shape[0]
    mspec = pl.BlockSpec((1, SMEM_BLOCK, MEM_W, MEM_TOKENS), lambda j: (l, j, 0, 0))
    return pl.pallas_call(
        _sample_mem_kernel,
        grid=(nseq // SMEM_BLOCK,),
        in_specs=[
            pl.BlockSpec((SMEM_BLOCK, SUBLANES, MEM_W), lambda j: (j, 0, 0)),
            mspec, mspec,
            pl.BlockSpec((1, 1, D_MODEL), lambda j: (l, 0, 0)),
        ],
        out_specs=pl.BlockSpec((SMEM_BLOCK, SUBLANES, MEM_W), lambda j: (j, 0, 0)),
        out_shape=jax.ShapeDtypeStruct((nseq, SUBLANES, MEM_W), F32),
        compiler_params=_params("parallel"),
        name="sample_mem",
    )(q, mem_k, mem_v, wts["norm_group"])


def _ffn_kernel(*refs, tm, s, has_carry, widths):
    it = iter(refs)
    x_ref = next(it)
    piece_refs = [next(it) for _ in widths]
    (wout_ref, npost_ref, npre_ref, wg_ref, wu_ref, fcw_ref, fcb_ref, wd_ref,
     npf_ref) = [next(it) for _ in range(9)]
    c0_ref = next(it) if has_carry else None
    y_ref, fnew_ref, gbuf = next(it), next(it), next(it)

    i = pl.program_id(1)
    base = _carry_base(s)

    att = None
    off = 0
    for pref, wd in zip(piece_refs, widths):
        part = _dot(pref[...], wout_ref[0, off:off + wd, :])
        att = part if att is None else att + part
        off += wd
    x1 = x_ref[...] + _rms(att, npost_ref[0])
    h2 = _rms(x1, npre_ref[0]).astype(BF16)
    g = _dot(h2, wg_ref[0])
    up = _dot(h2, wu_ref[0])

    @pl.when(i == 0)
    def _():
        if has_carry:
            gbuf[base - 2 * s:base, :] = c0_ref[0]
        else:
            gbuf[0:base, :] = jnp.zeros((base, D_FF), F32)

    gbuf[base:base + tm, :] = g
    w = fcw_ref[0]
    gc = (w[0:1] * gbuf[base - 2 * s:base - 2 * s + tm, :]
          + w[1:2] * gbuf[base - s:base - s + tm, :] + w[2:3] * g + fcb_ref[0])
    newc = gbuf[base + tm - 2 * s:base + tm, :]
    gbuf[base - 2 * s:base, :] = newc
    fnew_ref[0] = newc
    a = (gc * (1.0 / (1.0 + jnp.exp(-gc))) * up).astype(BF16)
    y_ref[...] = x1 + _rms(_dot(a, wd_ref[0]), npf_ref[0])


def _out_ffn(l, x, pieces, nb, t_len, tm, s, wts, carry0=None):
    nt = t_len // tm
    rows = nb * t_len
    has_carry = carry0 is not None
    widths = tuple(p.shape[1] for p in pieces)
    row_map = lambda b, i: (b * nt + i, 0)
    lay = lambda b, i: (l, 0, 0)
    in_specs = [pl.BlockSpec((tm, D_MODEL), row_map)]
    in_specs += [pl.BlockSpec((tm, wd), row_map) for wd in widths]
    in_specs += [
        _resident((1, D_MODEL, D_MODEL), lay),
        pl.BlockSpec((1, 1, D_MODEL), lay),
        pl.BlockSpec((1, 1, D_MODEL), lay),
        _resident((1, D_MODEL, D_FF), lay),
        _resident((1, D_MODEL, D_FF), lay),
        pl.BlockSpec((1, 3, D_FF), lay),
        pl.BlockSpec((1, 1, D_FF), lay),
        _resident((1, D_FF, D_MODEL), lay),
        pl.BlockSpec((1, 1, D_MODEL), lay),
    ]
    args = [x, *pieces, wts["w_out"], wts["norm_post_mix"], wts["norm_pre_ffn"], wts["w_gate"],
            wts["w_up"], wts["ffn_conv_w"], wts["ffn_conv_b"], wts["w_down"],
            wts["norm_post_ffn"]]
    if has_carry:
        in_specs.append(pl.BlockSpec((1, 2 * s, D_FF), lambda b, i: (b, 0, 0)))
        args.append(carry0)
    kern = functools.partial(_ffn_kernel, tm=tm, s=s, has_carry=has_carry, widths=widths)
    return pl.pallas_call(
        kern,
        grid=(nb, nt),
        in_specs=in_specs,
        out_specs=[pl.BlockSpec((tm, D_MODEL), row_map),
                   pl.BlockSpec((1, 2 * s, D_FF), lambda b, i: (b, 0, 0))],
        out_shape=[jax.ShapeDtypeStruct((rows, D_MODEL), F32),
                   jax.ShapeDtypeStruct((nb, 2 * s, D_FF), F32)],
        scratch_shapes=[pltpu.VMEM((_carry_base(s) + tm, D_FF), F32)],
        compiler_params=_params("parallel", "arbitrary"),
        name="out_ffn",
    )(*args)


def _rot_pair(w):
    half = MLA_ROPE // 2
    return jnp.concatenate([-w[..., half:], w[..., :half]], axis=-1)


def _prep_weights(norm_pre_mix, w_in, norm_q, w_uq, norm_kv, w_uk, w_uv, conv_w, norm_mem,
                  w_mem_k, w_mem_v, norm_group, w_out, norm_post_mix, norm_pre_ffn, w_gate, w_up,
                  ffn_conv_w, ffn_conv_b, w_down, norm_post_ffn):
    o0 = Q_LORA
    o1 = o0 + KV_LORA
    o2 = o1 + MLA_ROPE
    o3 = o2 + CONV_W
    o4 = o3 + CONV_W
    o5 = o4 + CONV_W
    kpe_w = w_in[..., o1:o2]
    w_in_r = jnp.concatenate(
        [w_in[..., :o1], w_in[..., o2:], kpe_w, _rot_pair(kpe_w),
         jnp.zeros(w_in.shape[:2] + (128 - 2 * MLA_ROPE,), w_in.dtype)], axis=-1).astype(BF16)

    per_head = w_uq.reshape(DEPTH, Q_LORA, MLA_HEADS, MLA_NOPE + MLA_ROPE)
    nope = per_head[..., :MLA_NOPE].reshape(DEPTH, Q_LORA, MLA_HEADS * MLA_NOPE)
    pe = per_head[..., MLA_NOPE:]
    w_uq_r = jnp.concatenate(
        [nope, pe.reshape(DEPTH, Q_LORA, -1), _rot_pair(pe).reshape(DEPTH, Q_LORA, -1)],
        axis=-1).astype(BF16)

    wt = jnp.transpose(w_uk, (0, 2, 3, 1)).reshape(DEPTH, MLA_HEADS // 2, 2, MLA_NOPE, KV_LORA)
    w_uk2 = jnp.einsum("ljanr,ab->ljanbr", wt, jnp.eye(2, dtype=wt.dtype)).reshape(
        DEPTH, MLA_HEADS // 2, 2 * MLA_NOPE, 2 * KV_LORA).astype(BF16)

    w_uv_exp = jnp.einsum("lrhv,hg->lhrgv", w_uv, jnp.eye(MLA_HEADS, dtype=w_uv.dtype)).reshape(
        DEPTH, MLA_HEADS, KV_LORA, MLA_W).astype(BF16)

    vec = lambda a: a.reshape(DEPTH, 1, a.shape[-1])
    return dict(
        norm_pre_mix=vec(norm_pre_mix), w_in=w_in_r, norm_q=vec(norm_q), w_uq=w_uq_r,
        norm_kv=vec(norm_kv), w_uk2=w_uk2,
        w_uv_t=jnp.transpose(w_uv, (0, 2, 3, 1)).astype(BF16),
        ng_attn=jnp.broadcast_to(norm_group[:, :MLA_W, None], (DEPTH, MLA_W, ATTN_TQ)),
        w_uv_flat=w_uv_exp.reshape(DEPTH, MLA_HEADS * KV_LORA, MLA_W), conv_w=conv_w,
        norm_mem=vec(norm_mem), w_mem_k=jnp.transpose(w_mem_k, (0, 2, 1)).astype(BF16),
        w_mem_v=jnp.transpose(w_mem_v, (0, 2, 1)).astype(BF16),
        norm_group=vec(norm_group), w_out=w_out.astype(BF16), norm_post_mix=vec(norm_post_mix),
        norm_pre_ffn=vec(norm_pre_ffn), w_gate=w_gate.astype(BF16), w_up=w_up.astype(BF16),
        ffn_conv_w=ffn_conv_w, ffn_conv_b=vec(ffn_conv_b), w_down=w_down.astype(BF16),
        norm_post_ffn=vec(norm_post_ffn))


def _rope_tables(pos):
    half = MLA_ROPE // 2
    freqs = ROPE_THETA ** (-jnp.arange(half, dtype=F32) * (2.0 / MLA_ROPE))
    ang = pos.astype(F32)[:, None] * freqs[None, :]
    cos = jnp.cos(ang)
    sin = jnp.sin(ang)
    cos32 = jnp.concatenate([cos, cos], axis=-1)
    sin32 = jnp.concatenate([sin, sin], axis=-1)
    tabk = jnp.concatenate([cos32, sin32, jnp.zeros((pos.shape[0], 64), F32)], axis=-1)
    return jnp.tile(cos32, (1, MLA_HEADS)), jnp.tile(sin32, (1, MLA_HEADS)), tabk


def _to_time_major(a):
    return jnp.transpose(a, (1, 0, 2)).reshape(-1, a.shape[-1])


def _from_time_major(a, t_len):
    return jnp.transpose(a.reshape(t_len, -1, a.shape[-1]), (1, 0, 2))


def kernel(x_prompt, x_sample, mem_prompt, cache_ckv, cache_kpe, cache_mem_k, cache_mem_v, state_conv, state_ffn_conv, page_table, norm_pre_mix, w_in, norm_q, w_uq, norm_kv, w_uk, w_uv, conv_w, norm_mem, w_mem_k, w_mem_v, norm_group, w_out, norm_post_mix, norm_pre_ffn, w_gate, w_up, ffn_conv_w, ffn_conv_b, w_down, norm_post_ffn):
    nb, seq, _ = x_prompt.shape
    nseq, dseq, _ = x_sample.shape
    n_pages = page_table.shape[1]
    page = cache_ckv.shape[2]
    past = n_pages * page
    wts = _prep_weights(norm_pre_mix, w_in, norm_q, w_uq, norm_kv, w_uk, w_uv, conv_w, norm_mem,
                        w_mem_k, w_mem_v, norm_group, w_out, norm_post_mix, norm_pre_ffn, w_gate,
                        w_up, ffn_conv_w, ffn_conv_b, w_down, norm_post_ffn)

    tabs_p = _rope_tables(jnp.arange(seq, dtype=jnp.int32))
    pos_s = jnp.repeat(past + jnp.arange(dseq, dtype=jnp.int32), nseq)
    tabs_s = _rope_tables(pos_s)

    mem_k_all, mem_v_all = _mem_kv(mem_prompt, wts["norm_mem"], wts["w_mem_k"], wts["w_mem_v"])
    smem_k = jnp.transpose(cache_mem_k, (0, 1, 3, 4, 2)).reshape(DEPTH, nseq, MEM_W, MEM_TOKENS)
    smem_v = jnp.transpose(cache_mem_v, (0, 1, 3, 4, 2)).reshape(DEPTH, nseq, MEM_W, MEM_TOKENS)
    cache_kpe_t = jnp.transpose(cache_kpe, (0, 1, 3, 2))

    yp = x_prompt.reshape(nb * seq, D_MODEL)
    ys = _to_time_major(x_sample)
    srows = nseq * dseq
    p_ckv, p_kpe, p_conv, p_ffn = [], [], [], []
    s_ckv, s_kpe, s_conv, s_ffn = [], [], [], []
    for l in range(DEPTH):
        qlat, qpe, kall, ckvt, ckv, kpt, mbc, cnew = _in_proj(
            l, yp, nb, seq, PROMPT_TM, 1, tabs_p, wts,
            mem_kv=(mem_k_all, mem_v_all))
        ma = _prompt_attention(l, qlat, qpe, kall, ckvt, wts)
        yp, fnew = _out_ffn(l, yp, [ma, mbc], nb, seq, PROMPT_TM, 1, wts)
        p_ckv.append(ckv.reshape(nb, seq, KV_LORA))
        p_kpe.append(kpt)
        p_conv.append(cnew)
        p_ffn.append(fnew)

        c0 = _to_time_major(state_conv[l])[None]
        f0 = _to_time_major(state_ffn_conv[l])[None]
        qlat, qpe, _, _, ckv, kpt, mb, qm, cnew = _in_proj(
            l, ys, 1, srows, srows, nseq, tabs_s, wts, carry0=c0)
        ckv_bt = _from_time_major(ckv, dseq)
        kp_dtb = kpt[0].reshape(MLA_ROPE, dseq, nseq)
        qlat_bt = jnp.transpose(qlat[0].reshape(MLA_HEADS, dseq, nseq, KV_LORA),
                                (2, 1, 0, 3)).reshape(nseq, dseq * MLA_HEADS, KV_LORA)
        qpe_bt = jnp.transpose(qpe[0].reshape(dseq, nseq, MLA_HEADS, MLA_ROPE),
                               (1, 0, 2, 3)).reshape(nseq, dseq * MLA_HEADS, MLA_ROPE)
        cnew16 = jnp.pad(ckv_bt, ((0, 0), (0, NEW_PAD - dseq), (0, 0)))
        pnew16 = jnp.pad(jnp.transpose(kp_dtb, (2, 0, 1)), ((0, 0), (0, 0), (0, NEW_PAD - dseq)))
        o_lat = _sample_attention(l, page_table, qlat_bt, qpe_bt, cnew16, pnew16,
                                  cache_ckv, cache_kpe_t)
        o_lat = _to_time_major(o_lat.reshape(nseq, dseq, MLA_HEADS * KV_LORA))
        ma = _uv_proj(l, o_lat, wts)
        qm_bt = jnp.pad(_from_time_major(qm, dseq), ((0, 0), (0, SUBLANES - dseq), (0, 0)))
        mc = _sample_mem(l, qm_bt, smem_k, smem_v, wts)
        mc = _to_time_major(mc[:, :dseq]).astype(BF16)
        ys, fnew = _out_ffn(l, ys, [ma, mb, mc], 1, srows, srows, nseq, wts, carry0=f0)
        s_ckv.append(ckv_bt)
        s_kpe.append(jnp.transpose(kp_dtb, (2, 1, 0)))
        s_conv.append(_from_time_major(cnew[0], 2))
        s_ffn.append(_from_time_major(fnew[0], 2))

    def mem_out(a):
        return jnp.transpose(a.reshape(DEPTH, nb, MEM_HEADS, HEAD_DIM, MEM_TOKENS), (0, 1, 4, 2, 3))

    return (yp.reshape(nb, seq, D_MODEL), _from_time_major(ys, dseq),
            jnp.stack(p_ckv), jnp.transpose(jnp.stack(p_kpe), (0, 1, 3, 2)),
            mem_out(mem_k_all), mem_out(mem_v_all),
            jnp.stack(p_conv), jnp.stack(p_ffn),
            jnp.stack(s_ckv), jnp.stack(s_kpe), jnp.stack(s_conv), jnp.stack(s_ffn))
```

```python
import functools

import jax
import jax.numpy as jnp
import numpy as np
from jax import lax
from jax.experimental import pallas as pl
from jax.experimental.pallas import tpu as pltpu

F32 = jnp.float32
BF16 = jnp.bfloat16

D_MODEL = 1024
DEPTH = 4
HEAD_DIM = 64
MLA_HEADS = 8
MLA_NOPE = 64
MLA_ROPE = 32
MLA_V = 64
MLA_W = MLA_HEADS * MLA_V
KV_LORA = 256
Q_LORA = 384
CONV_W = 256
MEM_HEADS = 4
MEM_W = MEM_HEADS * HEAD_DIM
MEM_TOKENS = 256
D_FF = 2816
ROPE_THETA = 10000.0
EPS = 1e-6
MLA_SCALE = (MLA_NOPE + MLA_ROPE) ** -0.5
MEM_SCALE = HEAD_DIM ** -0.5
NEG = -1e30
LOG2E = 1.4426950408889634

Z_CQ = 0
Z_CKV = Z_CQ + Q_LORA
Z_GB = Z_CKV + KV_LORA
Z_GC = Z_GB + CONV_W
Z_HV = Z_GC + CONV_W
Z_QM = Z_HV + CONV_W
Z_KPE = Z_QM + MEM_W
Z_COLS = Z_KPE + 128

SUBLANES = 8
VMEM_LIMIT_BYTES = 56 * 1024 * 1024

PROMPT_TM = 256
ATTN_TQ = 512
ATTN_TK = 512
PAGE_CHUNK = 2048
NEW_PAD = 16
FF_CHUNK = 256
FFN_SUBTILES = 2
FFN_TM = 512
SMEM_BLOCK = 8
DMA_UNROLL = 8


def _rms(x, g):
    return x * lax.rsqrt(jnp.mean(x * x, axis=-1, keepdims=True) + EPS) * g


def _dot(a, b):
    return jnp.dot(a, b, preferred_element_type=F32)


def _dot_nt(a, b):
    return lax.dot_general(a, b, (((1,), (1,)), ((), ())), preferred_element_type=F32)


def _resident(shape, index_map):
    return pl.BlockSpec(shape, index_map, pipeline_mode=pl.Buffered(1))


def _params(*sem):
    return pltpu.CompilerParams(dimension_semantics=sem, vmem_limit_bytes=VMEM_LIMIT_BYTES)


def _head_mask(lane, h, width):
    return (lane >= width * h) & (lane < width * (h + 1))


def _carry_base(s):
    return -(-2 * s // SUBLANES) * SUBLANES


def _mem_kv_kernel(mem_ref, g_ref, wk_ref, wv_ref, k_ref, v_ref):
    m = _rms(mem_ref[0], g_ref[0]).astype(BF16)
    k_ref[0, 0] = _dot_nt(wk_ref[0], m)
    v_ref[0, 0] = _dot_nt(wv_ref[0], m)


def _mem_kv(mem, norm_mem, wk, wv):
    nb = mem.shape[0]
    out = jax.ShapeDtypeStruct((DEPTH, nb, MEM_W, MEM_TOKENS), F32)
    return pl.pallas_call(
        _mem_kv_kernel,
        grid=(DEPTH, nb),
        in_specs=[
            pl.BlockSpec((1, MEM_TOKENS, D_MODEL), lambda l, b: (b, 0, 0)),
            pl.BlockSpec((1, 1, D_MODEL), lambda l, b: (l, 0, 0)),
            pl.BlockSpec((1, MEM_W, D_MODEL), lambda l, b: (l, 0, 0)),
            pl.BlockSpec((1, MEM_W, D_MODEL), lambda l, b: (l, 0, 0)),
        ],
        out_specs=[
            pl.BlockSpec((1, 1, MEM_W, MEM_TOKENS), lambda l, b: (l, b, 0, 0)),
            pl.BlockSpec((1, 1, MEM_W, MEM_TOKENS), lambda l, b: (l, b, 0, 0)),
        ],
        out_shape=[out, out],
        compiler_params=_params("parallel", "parallel"),
        name="mem_kv",
    )(mem, norm_mem, wk, wv)


def _mem_attend(qm, mk_t, mv_t, rows):
    lane = lax.broadcasted_iota(jnp.int32, (rows, MEM_W), 1)
    masks = [_head_mask(lane, h, HEAD_DIM) for h in range(MEM_HEADS)]
    qs = jnp.concatenate([jnp.where(m, qm, 0.0) for m in masks], axis=0).astype(BF16)
    sc = _dot(qs, mk_t.astype(BF16)) * MEM_SCALE
    p = jnp.exp(sc - jnp.max(sc, axis=-1, keepdims=True))
    o = _dot_nt(p.astype(BF16), mv_t.astype(BF16)) / jnp.sum(p, axis=-1, keepdims=True)
    om = jnp.where(masks[0], o[0:rows], 0.0)
    for h in range(1, MEM_HEADS):
        om = om + jnp.where(masks[h], o[h * rows:(h + 1) * rows], 0.0)
    return om


def _in_proj_kernel(*refs, tm, s, has_carry, do_mem):
    it = iter(refs)
    (x_ref, cosq_ref, sinq_ref, tabk_ref, npm_ref, win_ref, nq_ref, wuq_ref, nkv_ref,
     wuk_ref, cw_ref, ng_ref) = [next(it) for _ in range(12)]
    c0_ref = next(it) if has_carry else None
    if do_mem:
        mk_ref, mv_ref = next(it), next(it)
    qlat_ref, qpe_ref, kall_ref, ckvt_ref, ckv_ref, kpe_ref = [next(it) for _ in range(6)]
    if do_mem:
        mbc_ref = next(it)
    else:
        mb_ref, qm_ref = next(it), next(it)
    cnew_ref = next(it)
    ubuf = next(it)

    i = pl.program_id(1)
    base = _carry_base(s)

    h = _rms(x_ref[...], npm_ref[0]).astype(BF16)
    z = _dot(h, win_ref[0])

    cqn = _rms(z[:, Z_CQ:Z_CKV], nq_ref[0]).astype(BF16)
    q = _dot(cqn, wuq_ref[0])
    qpe = q[:, 512:768] * cosq_ref[...] + q[:, 768:1024] * sinq_ref[...]
    qpe_ref[0] = qpe.astype(BF16)
    for j in range(MLA_HEADS // 2):
        ql = _dot(q[:, 128 * j:128 * (j + 1)].astype(BF16), wuk_ref[0, j])
        qlat_ref[0, 2 * j] = ql[:, :KV_LORA].astype(BF16)
        qlat_ref[0, 2 * j + 1] = ql[:, KV_LORA:].astype(BF16)

    ckv = _rms(z[:, Z_CKV:Z_GB], nkv_ref[0])
    ckv_ref[...] = ckv
    ckvt_ref[0] = ckv.T.astype(BF16)
    t = z[:, Z_KPE:Z_COLS] * tabk_ref[...]
    r = t + pltpu.roll(t, 96, 1)
    lane = lax.broadcasted_iota(jnp.int32, (tm, 128), 1)
    kp = jnp.where(lane < MLA_ROPE, r, 0.0)
    kpe_ref[0] = kp.T[0:MLA_ROPE, :]
    kp = kp + pltpu.roll(kp, 32, 1)
    kp = (kp + pltpu.roll(kp, 64, 1)).astype(BF16)
    kall_ref[0, :, 0:KV_LORA] = ckv.astype(BF16)
    kall_ref[0, :, KV_LORA:KV_LORA + 128] = kp
    kall_ref[0, :, KV_LORA + 128:KV_LORA + 256] = kp

    u = z[:, Z_GC:Z_HV] * z[:, Z_HV:Z_QM]

    @pl.when(i == 0)
    def _():
        if has_carry:
            ubuf[base - 2 * s:base, :] = c0_ref[0]
        else:
            ubuf[0:base, :] = jnp.zeros((base, CONV_W), F32)

    ubuf[base:base + tm, :] = u
    w = cw_ref[0]
    cv = (w[0:1] * ubuf[base - 2 * s:base - 2 * s + tm, :]
          + w[1:2] * ubuf[base - s:base - s + tm, :] + w[2:3] * u)
    newc = ubuf[base + tm - 2 * s:base + tm, :]
    ubuf[base - 2 * s:base, :] = newc
    cnew_ref[0] = newc
    ng = ng_ref[0]
    mb = _rms(z[:, Z_GB:Z_GC] * cv, ng[:, MLA_W:MLA_W + CONV_W]).astype(BF16)

    qm = z[:, Z_QM:Z_KPE]
    if do_mem:
        om = _mem_attend(qm, mk_ref[0, 0], mv_ref[0, 0], tm)
        mbc_ref[:, 0:CONV_W] = mb
        mbc_ref[:, CONV_W:] = _rms(om, ng[:, MLA_W + CONV_W:]).astype(BF16)
    else:
        mb_ref[...] = mb
        qm_ref[...] = qm


def _in_proj(l, x, nb, t_len, tm, s, tabs, wts, carry0=None, mem_kv=None):
    nt = t_len // tm
    rows = nb * t_len
    has_carry = carry0 is not None
    do_mem = mem_kv is not None
    row_map = lambda b, i: (b * nt + i, 0)
    tab_map = lambda b, i: (i, 0)
    lay = lambda b, i: (l, 0, 0)
    in_specs = [
        pl.BlockSpec((tm, D_MODEL), row_map),
        pl.BlockSpec((tm, 256), tab_map),
        pl.BlockSpec((tm, 256), tab_map),
        pl.BlockSpec((tm, 128), tab_map),
        pl.BlockSpec((1, 1, D_MODEL), lay),
        _resident((1, D_MODEL, Z_COLS), lay),
        pl.BlockSpec((1, 1, Q_LORA), lay),
        _resident((1, Q_LORA, 1024), lay),
        pl.BlockSpec((1, 1, KV_LORA), lay),
        _resident((1, MLA_HEADS // 2, 128, 2 * KV_LORA), lambda b, i: (l, 0, 0, 0)),
        pl.BlockSpec((1, 3, CONV_W), lay),
        pl.BlockSpec((1, 1, D_MODEL), lay),
    ]
    args = [x, *tabs, wts["norm_pre_mix"], wts["w_in"], wts["norm_q"], wts["w_uq"],
            wts["norm_kv"], wts["w_uk2"], wts["conv_w"], wts["norm_group"]]
    if has_carry:
        in_specs.append(pl.BlockSpec((1, 2 * s, CONV_W), lambda b, i: (b, 0, 0)))
        args.append(carry0)
    if do_mem:
        mspec = pl.BlockSpec((1, 1, MEM_W, MEM_TOKENS), lambda b, i: (l, b, 0, 0))
        in_specs += [mspec, mspec]
        args += list(mem_kv)
    out_shape = [
        jax.ShapeDtypeStruct((nb, MLA_HEADS, t_len, KV_LORA), BF16),
        jax.ShapeDtypeStruct((nb, t_len, 256), BF16),
        jax.ShapeDtypeStruct((nb, t_len, 512), BF16),
        jax.ShapeDtypeStruct((nb, KV_LORA, t_len), BF16),
        jax.ShapeDtypeStruct((rows, KV_LORA), F32),
        jax.ShapeDtypeStruct((nb, MLA_ROPE, t_len), F32),
    ]
    out_specs = [
        pl.BlockSpec((1, MLA_HEADS, tm, KV_LORA), lambda b, i: (b, 0, i, 0)),
        pl.BlockSpec((1, tm, 256), lambda b, i: (b, i, 0)),
        pl.BlockSpec((1, tm, 512), lambda b, i: (b, i, 0)),
        pl.BlockSpec((1, KV_LORA, tm), lambda b, i: (b, 0, i)),
        pl.BlockSpec((tm, KV_LORA), row_map),
        pl.BlockSpec((1, MLA_ROPE, tm), lambda b, i: (b, 0, i)),
    ]
    if do_mem:
        out_shape.append(jax.ShapeDtypeStruct((rows, 512), BF16))
        out_specs.append(pl.BlockSpec((tm, 512), row_map))
    else:
        out_shape += [jax.ShapeDtypeStruct((rows, CONV_W), BF16),
                      jax.ShapeDtypeStruct((rows, MEM_W), F32)]
        out_specs += [pl.BlockSpec((tm, CONV_W), row_map), pl.BlockSpec((tm, MEM_W), row_map)]
    out_shape.append(jax.ShapeDtypeStruct((nb, 2 * s, CONV_W), F32))
    out_specs.append(pl.BlockSpec((1, 2 * s, CONV_W), lambda b, i: (b, 0, 0)))
    kern = functools.partial(_in_proj_kernel, tm=tm, s=s, has_carry=has_carry, do_mem=do_mem)
    return pl.pallas_call(
        kern,
        grid=(nb, nt),
        in_specs=in_specs,
        out_specs=out_specs,
        out_shape=out_shape,
        scratch_shapes=[pltpu.VMEM((_carry_base(s) + tm, CONV_W), F32)],
        compiler_params=_params("parallel", "arbitrary"),
        name="in_proj_mem" if do_mem else "in_proj",
    )(*args)


def _causal_pairs(nq, tq, tk):
    it, kt = [], []
    for i in range(nq):
        for k in range((i * tq + tq - 1) // tk + 1):
            it.append(i)
            kt.append(k)
    return np.asarray(it, np.int32), np.asarray(kt, np.int32)


def _attn_kernel(it_ref, kt_ref, qlat_ref, qpe_ref, kall_ref, ckvt_ref, wuvt_ref, ngb_ref, o_ref,
                 q2, m_scr, l_scr, acct, *, tq, tk):
    j = pl.program_id(1)
    i = it_ref[j]
    k = kt_ref[j]
    last_k = (i * tq + tq - 1) // tk
    cols = MLA_HEADS * tq
    c_exp = MLA_SCALE * LOG2E

    @pl.when(k == 0)
    def _():
        lane = lax.broadcasted_iota(jnp.int32, (tq, 256), 1)
        qp = qpe_ref[0]
        zero = jnp.zeros_like(qp)
        for h in range(MLA_HEADS):
            q2[h * tq:(h + 1) * tq, 0:KV_LORA] = qlat_ref[0, h]
            q2[h * tq:(h + 1) * tq, KV_LORA:] = jnp.where(_head_mask(lane, h, MLA_ROPE), qp, zero)
        m_scr[...] = jnp.full((1, cols), NEG, F32)
        l_scr[...] = jnp.zeros((1, cols), F32)
        acct[...] = jnp.zeros((KV_LORA, cols), F32)

    def scores():
        return _dot_nt(kall_ref[0], q2[...])

    def step(sc):
        m_prev = m_scr[...]
        m_new = jnp.maximum(m_prev, jnp.max(sc, axis=0, keepdims=True))
        alpha = jnp.exp2((m_prev - m_new) * c_exp)
        p = jnp.exp2((sc - m_new) * c_exp)
        l_scr[...] = alpha * l_scr[...] + jnp.sum(p, axis=0, keepdims=True)
        acct[...] = alpha * acct[...] + _dot(ckvt_ref[0], p.astype(BF16))
        m_scr[...] = m_new

    @pl.when(k < last_k)
    def _():
        step(scores())

    @pl.when(k == last_k)
    def _():
        k_pos = k * tk + lax.broadcasted_iota(jnp.int32, (tk, cols), 0)
        q_pos = i * tq + (lax.broadcasted_iota(jnp.int32, (tk, cols), 1) & (tq - 1))
        step(jnp.where(k_pos <= q_pos, scores(), NEG))
        ot = (acct[...] * (1.0 / l_scr[...])).astype(BF16)
        omt = jnp.concatenate(
            [_dot(wuvt_ref[0, h], ot[:, h * tq:(h + 1) * tq]) for h in range(MLA_HEADS)], axis=0)
        y = omt * lax.rsqrt(jnp.mean(omt * omt, axis=0, keepdims=True) + EPS) * ngb_ref[0]
        o_ref[...] = y.T.astype(BF16)


def _prompt_attention(l, qlat, qpe, kall, ckvt, wts):
    nb, _, t_len, _ = qlat.shape
    tq, tk = ATTN_TQ, ATTN_TK
    nq = t_len // tq
    cols = MLA_HEADS * tq
    it, kt = _causal_pairs(nq, tq, tk)
    kern = functools.partial(_attn_kernel, tq=tq, tk=tk)
    grid_spec = pltpu.PrefetchScalarGridSpec(
        num_scalar_prefetch=2,
        grid=(nb, it.shape[0]),
        in_specs=[
            pl.BlockSpec((1, MLA_HEADS, tq, KV_LORA), lambda b, j, it, kt: (b, 0, it[j], 0)),
            pl.BlockSpec((1, tq, 256), lambda b, j, it, kt: (b, it[j], 0)),
            pl.BlockSpec((1, tk, 512), lambda b, j, it, kt: (b, kt[j], 0)),
            pl.BlockSpec((1, KV_LORA, tk), lambda b, j, it, kt: (b, 0, kt[j])),
            _resident((1, MLA_HEADS, MLA_V, KV_LORA), lambda b, j, it, kt: (l, 0, 0, 0)),
            pl.BlockSpec((1, MLA_W, tq), lambda b, j, it, kt: (l, 0, 0)),
        ],
        out_specs=pl.BlockSpec((tq, MLA_W), lambda b, j, it, kt: (b * nq + it[j], 0)),
        scratch_shapes=[
            pltpu.VMEM((cols, 512), BF16),
            pltpu.VMEM((1, cols), F32),
            pltpu.VMEM((1, cols), F32),
            pltpu.VMEM((KV_LORA, cols), F32),
        ],
    )
    return pl.pallas_call(
        kern,
        grid_spec=grid_spec,
        out_shape=jax.ShapeDtypeStruct((nb * t_len, MLA_W), BF16),
        compiler_params=_params("parallel", "arbitrary"),
        name="prompt_attn",
    )(jnp.asarray(it), jnp.asarray(kt), qlat, qpe, kall, ckvt, wts["w_uv_t"], wts["ng_attn"])


def _sample_attn_kernel(pt_ref, qlat_ref, qpe_ref, cnew_ref, pnew_ref, ckv_hbm, kpe_hbm, o_ref,
                        kbuf, pbuf, kb16, s_scr, sem, *, l, n_pages, page):
    b = pl.program_id(0)
    nb = pl.num_programs(0)
    slot = b % 2
    past = n_pages * page

    def page_copies(seq, sl, p):
        pg = pt_ref[seq, p]
        dst = pl.ds(pl.multiple_of(p * page, page), page)
        return (pltpu.make_async_copy(ckv_hbm.at[l, pg], kbuf.at[sl, dst, :], sem.at[0, sl]),
                pltpu.make_async_copy(kpe_hbm.at[l, pg], pbuf.at[sl, :, dst], sem.at[1, sl]))

    def issue(seq, sl):
        def body(p, c):
            for cp in page_copies(seq, sl, p):
                cp.start()
            return c
        lax.fori_loop(0, n_pages, body, 0, unroll=DMA_UNROLL)

    @pl.when(b == 0)
    def _():
        issue(0, 0)

    @pl.when(b + 1 < nb)
    def _():
        issue(b + 1, 1 - slot)

    first = pl.ds(0, page)
    for _ in range(n_pages):
        pltpu.make_async_copy(ckv_hbm.at[l, 0], kbuf.at[slot, first, :], sem.at[0, slot]).wait()
    for _ in range(n_pages):
        pltpu.make_async_copy(kpe_hbm.at[l, 0], pbuf.at[slot, :, first], sem.at[1, slot]).wait()

    q = qlat_ref[0]
    qp = qpe_ref[0]
    nrow = q.shape[0]
    nch = past // PAGE_CHUNK
    for c in range(nch):
        sl_c = slice(c * PAGE_CHUNK, (c + 1) * PAGE_CHUNK)
        kb16[sl_c, :] = kbuf[slot, sl_c, :].astype(BF16)
    s_scr[...] = (_dot_nt(q, kb16[...]) + _dot(qp, pbuf[slot].astype(BF16))) * MLA_SCALE

    @pl.when(b >= 0)
    def _():
        cn = cnew_ref[0].astype(BF16)
        pn = pnew_ref[0].astype(BF16)
        s_new = (_dot_nt(q, cn) + _dot(qp, pn)) * MLA_SCALE
        t_row = lax.broadcasted_iota(jnp.int32, (nrow, NEW_PAD), 0) // MLA_HEADS
        j_col = lax.broadcasted_iota(jnp.int32, (nrow, NEW_PAD), 1)
        s_new = jnp.where(j_col <= t_row, s_new, NEG)
        s_all = s_scr[...]
        m = jnp.maximum(jnp.max(s_all, axis=-1, keepdims=True),
                        jnp.max(s_new, axis=-1, keepdims=True))
        p_new = jnp.exp(s_new - m)
        p_all = jnp.exp(s_all - m)
        denom = jnp.sum(p_all, axis=-1, keepdims=True) + jnp.sum(p_new, axis=-1, keepdims=True)
        s_scr[...] = p_all
        accs = [_dot(p_new.astype(BF16), cn), None]
        for c in range(nch):
            sl_c = slice(c * PAGE_CHUNK, (c + 1) * PAGE_CHUNK)
            d = _dot(s_scr[:, sl_c].astype(BF16), kb16[sl_c, :])
            accs[c % 2] = d if accs[c % 2] is None else accs[c % 2] + d
        acc = accs[0] if accs[1] is None else accs[0] + accs[1]
        o_ref[0] = acc / denom


def _sample_attention(l, page_table, qlat, qpe, cnew, pnew_t, cache_ckv, cache_kpe_t):
    nseq, nrow, _ = qlat.shape
    n_pages = page_table.shape[1]
    page = cache_ckv.shape[2]
    past = n_pages * page
    kern = functools.partial(_sample_attn_kernel, l=l, n_pages=n_pages, page=page)
    grid_spec = pltpu.PrefetchScalarGridSpec(
        num_scalar_prefetch=1,
        grid=(nseq,),
        in_specs=[
            pl.BlockSpec((1, nrow, KV_LORA), lambda b, pt: (b, 0, 0)),
            pl.BlockSpec((1, nrow, MLA_ROPE), lambda b, pt: (b, 0, 0)),
            pl.BlockSpec((1, NEW_PAD, KV_LORA), lambda b, pt: (b, 0, 0)),
            pl.BlockSpec((1, MLA_ROPE, NEW_PAD), lambda b, pt: (b, 0, 0)),
            pl.BlockSpec(memory_space=pl.ANY),
            pl.BlockSpec(memory_space=pl.ANY),
        ],
        out_specs=pl.BlockSpec((1, nrow, KV_LORA), lambda b, pt: (b, 0, 0)),
        scratch_shapes=[
            pltpu.VMEM((2, past, KV_LORA), F32),
            pltpu.VMEM((2, MLA_ROPE, past), F32),
            pltpu.VMEM((past, KV_LORA), BF16),
            pltpu.VMEM((nrow, past), F32),
            pltpu.SemaphoreType.DMA((2, 2)),
        ],
    )
    return pl.pallas_call(
        kern,
        grid_spec=grid_spec,
        out_shape=jax.ShapeDtypeStruct((nseq, nrow, KV_LORA), F32),
        compiler_params=_params("arbitrary"),
        name="sample_attn",
    )(page_table, qlat, qpe, cnew, pnew_t, cache_ckv, cache_kpe_t)


def _uv_kernel(o_ref, w_ref, ng_ref, out_ref):
    om = _dot(o_ref[...].astype(BF16), w_ref[0])
    out_ref[...] = _rms(om, ng_ref[0][:, 0:MLA_W]).astype(BF16)


def _uv_proj(l, o_lat, wts):
    rows = o_lat.shape[0]
    return pl.pallas_call(
        _uv_kernel,
        grid=(1,),
        in_specs=[
            pl.BlockSpec((rows, MLA_HEADS * KV_LORA), lambda i: (0, 0)),
            pl.BlockSpec((1, MLA_HEADS * KV_LORA, MLA_W), lambda i: (l, 0, 0)),
            pl.BlockSpec((1, 1, D_MODEL), lambda i: (l, 0, 0)),
        ],
        out_specs=pl.BlockSpec((rows, MLA_W), lambda i: (0, 0)),
        out_shape=jax.ShapeDtypeStruct((rows, MLA_W), BF16),
        compiler_params=_params("arbitrary"),
        name="uv_proj",
    )(o_lat, wts["w_uv_flat"], wts["norm_group"])


def _sample_mem_kernel(q_ref, mk_ref, mv_ref, ng_ref, o_ref):
    ng = ng_ref[0]
    for j in range(SMEM_BLOCK):
        om = _mem_attend(q_ref[j], mk_ref[0, j], mv_ref[0, j], SUBLANES)
        o_ref[j] = _rms(om, ng[:, MLA_W + CONV_W:])


def _sample_mem(l, q, mem_k, mem_v, wts):
    nseq = q.shape[0]
    mspec = pl.BlockSpec((1, SMEM_BLOCK, MEM_W, MEM_TOKENS), lambda j: (l, j, 0, 0))
    return pl.pallas_call(
        _sample_mem_kernel,
        grid=(nseq // SMEM_BLOCK,),
        in_specs=[
            pl.BlockSpec((SMEM_BLOCK, SUBLANES, MEM_W), lambda j: (j, 0, 0)),
            mspec, mspec,
            pl.BlockSpec((1, 1, D_MODEL), lambda j: (l, 0, 0)),
        ],
        out_specs=pl.BlockSpec((SMEM_BLOCK, SUBLANES, MEM_W), lambda j: (j, 0, 0)),
        out_shape=jax.ShapeDtypeStruct((nseq, SUBLANES, MEM_W), F32),
        compiler_params=_params("parallel"),
        name="sample_mem",
    )(q, mem_k, mem_v, wts["norm_group"])


def _ffn_kernel(*refs, tm, s, has_carry, widths):
    it = iter(refs)
    x_ref = next(it)
    piece_refs = [next(it) for _ in widths]
    (wout_ref, npost_ref, npre_ref, wg_ref, wu_ref, fcw_ref, fcb_ref, wd_ref,
     npf_ref) = [next(it) for _ in range(9)]
    c0_ref = next(it) if has_carry else None
    y_ref, fnew_ref, gbuf = next(it), next(it), next(it)

    i = pl.program_id(1)
    base = _carry_base(s)
    rs = tm // FFN_SUBTILES
    nchunk = D_FF // FF_CHUNK
    cols = lambda c: slice(c * FF_CHUNK, (c + 1) * FF_CHUNK)

    @pl.when(i == 0)
    def _():
        if has_carry:
            gbuf[base - 2 * s:base, :] = c0_ref[0]
        else:
            gbuf[0:base, :] = jnp.zeros((base, D_FF), F32)

    w = fcw_ref[0]
    bias = fcb_ref[0]

    def prologue(t):
        rows = slice(t * rs, (t + 1) * rs)
        att = None
        off = 0
        for pref, wd in zip(piece_refs, widths):
            part = _dot(pref[rows, :], wout_ref[0, off:off + wd, :])
            att = part if att is None else att + part
            off += wd
        x1 = x_ref[rows, :] + _rms(att, npost_ref[0])
        return x1, _rms(x1, npre_ref[0]).astype(BF16)

    def gate_up(h2, c):
        return _dot(h2, wg_ref[0, :, cols(c)]), _dot(h2, wu_ref[0, :, cols(c)])

    def act(t, c, g, up):
        r0 = base + t * rs
        cs = cols(c)
        gbuf[r0:r0 + rs, cs] = g
        gc = (w[0:1, cs] * gbuf[r0 - 2 * s:r0 - 2 * s + rs, cs]
              + w[1:2, cs] * gbuf[r0 - s:r0 - s + rs, cs] + w[2:3, cs] * g + bias[:, cs])
        return (gc * (1.0 / (1.0 + jnp.exp(-gc))) * up).astype(BF16)

    x1s, h2s = [None] * FFN_SUBTILES, [None] * FFN_SUBTILES
    x1s[0], h2s[0] = prologue(0)
    items = [(t, c) for t in range(FFN_SUBTILES) for c in range(nchunk)]
    nxt = gate_up(h2s[0], 0)
    dn = None
    for idx, (t, c) in enumerate(items):
        g, up = nxt
        if idx == 0:
            for t2 in range(1, FFN_SUBTILES):
                x1s[t2], h2s[t2] = prologue(t2)
        if idx + 1 < len(items):
            t_n, c_n = items[idx + 1]
            nxt = gate_up(h2s[t_n], c_n)
        part = _dot(act(t, c, g, up), wd_ref[0, cols(c), :])
        dn = part if c == 0 else dn + part
        if c == nchunk - 1:
            y_ref[t * rs:(t + 1) * rs, :] = x1s[t] + _rms(dn, npf_ref[0])
    newc = gbuf[base + tm - 2 * s:base + tm, :]
    gbuf[base - 2 * s:base, :] = newc
    fnew_ref[0] = newc


def _out_ffn(l, x, pieces, nb, t_len, tm, s, wts, carry0=None):
    nt = t_len // tm
    rows = nb * t_len
    has_carry = carry0 is not None
    widths = tuple(p.shape[1] for p in pieces)
    row_map = lambda b, i: (b * nt + i, 0)
    lay = lambda b, i: (l, 0, 0)
    in_specs = [pl.BlockSpec((tm, D_MODEL), row_map)]
    in_specs += [pl.BlockSpec((tm, wd), row_map) for wd in widths]
    in_specs += [
        _resident((1, D_MODEL, D_MODEL), lay),
        pl.BlockSpec((1, 1, D_MODEL), lay),
        pl.BlockSpec((1, 1, D_MODEL), lay),
        _resident((1, D_MODEL, D_FF), lay),
        _resident((1, D_MODEL, D_FF), lay),
        pl.BlockSpec((1, 3, D_FF), lay),
        pl.BlockSpec((1, 1, D_FF), lay),
        _resident((1, D_FF, D_MODEL), lay),
        pl.BlockSpec((1, 1, D_MODEL), lay),
    ]
    args = [x, *pieces, wts["w_out"], wts["norm_post_mix"], wts["norm_pre_ffn"], wts["w_gate"],
            wts["w_up"], wts["ffn_conv_w"], wts["ffn_conv_b"], wts["w_down"],
            wts["norm_post_ffn"]]
    if has_carry:
        in_specs.append(pl.BlockSpec((1, 2 * s, D_FF), lambda b, i: (b, 0, 0)))
        args.append(carry0)
    kern = functools.partial(_ffn_kernel, tm=tm, s=s, has_carry=has_carry, widths=widths)
    return pl.pallas_call(
        kern,
        grid=(nb, nt),
        in_specs=in_specs,
        out_specs=[pl.BlockSpec((tm, D_MODEL), row_map),
                   pl.BlockSpec((1, 2 * s, D_FF), lambda b, i: (b, 0, 0))],
        out_shape=[jax.ShapeDtypeStruct((rows, D_MODEL), F32),
                   jax.ShapeDtypeStruct((nb, 2 * s, D_FF), F32)],
        scratch_shapes=[pltpu.VMEM((_carry_base(s) + tm, D_FF), F32)],
        compiler_params=_params("parallel", "arbitrary"),
        name="out_ffn",
    )(*args)


def _rot_pair(w):
    half = MLA_ROPE // 2
    return jnp.concatenate([-w[..., half:], w[..., :half]], axis=-1)


def _prep_weights(norm_pre_mix, w_in, norm_q, w_uq, norm_kv, w_uk, w_uv, conv_w, norm_mem,
                  w_mem_k, w_mem_v, norm_group, w_out, norm_post_mix, norm_pre_ffn, w_gate, w_up,
                  ffn_conv_w, ffn_conv_b, w_down, norm_post_ffn):
    o0 = Q_LORA
    o1 = o0 + KV_LORA
    o2 = o1 + MLA_ROPE
    o3 = o2 + CONV_W
    o4 = o3 + CONV_W
    o5 = o4 + CONV_W
    kpe_w = w_in[..., o1:o2]
    w_in_r = jnp.concatenate(
        [w_in[..., :o1], w_in[..., o2:], kpe_w, _rot_pair(kpe_w),
         jnp.zeros(w_in.shape[:2] + (128 - 2 * MLA_ROPE,), w_in.dtype)], axis=-1).astype(BF16)

    per_head = w_uq.reshape(DEPTH, Q_LORA, MLA_HEADS, MLA_NOPE + MLA_ROPE)
    nope = per_head[..., :MLA_NOPE].reshape(DEPTH, Q_LORA, MLA_HEADS * MLA_NOPE)
    pe = per_head[..., MLA_NOPE:]
    w_uq_r = jnp.concatenate(
        [nope, pe.reshape(DEPTH, Q_LORA, -1), _rot_pair(pe).reshape(DEPTH, Q_LORA, -1)],
        axis=-1).astype(BF16)

    wt = jnp.transpose(w_uk, (0, 2, 3, 1)).reshape(DEPTH, MLA_HEADS // 2, 2, MLA_NOPE, KV_LORA)
    w_uk2 = jnp.einsum("ljanr,ab->ljanbr", wt, jnp.eye(2, dtype=wt.dtype)).reshape(
        DEPTH, MLA_HEADS // 2, 2 * MLA_NOPE, 2 * KV_LORA).astype(BF16)

    w_uv_exp = jnp.einsum("lrhv,hg->lhrgv", w_uv, jnp.eye(MLA_HEADS, dtype=w_uv.dtype)).reshape(
        DEPTH, MLA_HEADS, KV_LORA, MLA_W).astype(BF16)

    vec = lambda a: a.reshape(DEPTH, 1, a.shape[-1])
    return dict(
        norm_pre_mix=vec(norm_pre_mix), w_in=w_in_r, norm_q=vec(norm_q), w_uq=w_uq_r,
        norm_kv=vec(norm_kv), w_uk2=w_uk2,
        w_uv_t=jnp.transpose(w_uv, (0, 2, 3, 1)).astype(BF16),
        ng_attn=jnp.broadcast_to(norm_group[:, :MLA_W, None], (DEPTH, MLA_W, ATTN_TQ)),
        w_uv_flat=w_uv_exp.reshape(DEPTH, MLA_HEADS * KV_LORA, MLA_W), conv_w=conv_w,
        norm_mem=vec(norm_mem), w_mem_k=jnp.transpose(w_mem_k, (0, 2, 1)).astype(BF16),
        w_mem_v=jnp.transpose(w_mem_v, (0, 2, 1)).astype(BF16),
        norm_group=vec(norm_group), w_out=w_out.astype(BF16), norm_post_mix=vec(norm_post_mix),
        norm_pre_ffn=vec(norm_pre_ffn), w_gate=w_gate.astype(BF16), w_up=w_up.astype(BF16),
        ffn_conv_w=ffn_conv_w, ffn_conv_b=vec(ffn_conv_b), w_down=w_down.astype(BF16),
        norm_post_ffn=vec(norm_post_ffn))


def _rope_tables(pos):
    half = MLA_ROPE // 2
    freqs = ROPE_THETA ** (-jnp.arange(half, dtype=F32) * (2.0 / MLA_ROPE))
    ang = pos.astype(F32)[:, None] * freqs[None, :]
    cos = jnp.cos(ang)
    sin = jnp.sin(ang)
    cos32 = jnp.concatenate([cos, cos], axis=-1)
    sin32 = jnp.concatenate([sin, sin], axis=-1)
    tabk = jnp.concatenate([cos32, sin32, jnp.zeros((pos.shape[0], 64), F32)], axis=-1)
    return jnp.tile(cos32, (1, MLA_HEADS)), jnp.tile(sin32, (1, MLA_HEADS)), tabk


def _to_time_major(a):
    return jnp.transpose(a, (1, 0, 2)).reshape(-1, a.shape[-1])


def _from_time_major(a, t_len):
    return jnp.transpose(a.reshape(t_len, -1, a.shape[-1]), (1, 0, 2))


def kernel(x_prompt, x_sample, mem_prompt, cache_ckv, cache_kpe, cache_mem_k, cache_mem_v, state_conv, state_ffn_conv, page_table, norm_pre_mix, w_in, norm_q, w_uq, norm_kv, w_uk, w_uv, conv_w, norm_mem, w_mem_k, w_mem_v, norm_group, w_out, norm_post_mix, norm_pre_ffn, w_gate, w_up, ffn_conv_w, ffn_conv_b, w_down, norm_post_ffn):
    nb, seq, _ = x_prompt.shape
    nseq, dseq, _ = x_sample.shape
    n_pages = page_table.shape[1]
    page = cache_ckv.shape[2]
    past = n_pages * page
    wts = _prep_weights(norm_pre_mix, w_in, norm_q, w_uq, norm_kv, w_uk, w_uv, conv_w, norm_mem,
                        w_mem_k, w_mem_v, norm_group, w_out, norm_post_mix, norm_pre_ffn, w_gate,
                        w_up, ffn_conv_w, ffn_conv_b, w_down, norm_post_ffn)

    tabs_p = _rope_tables(jnp.arange(seq, dtype=jnp.int32))
    pos_s = jnp.repeat(past + jnp.arange(dseq, dtype=jnp.int32), nseq)
    tabs_s = _rope_tables(pos_s)

    mem_k_all, mem_v_all = _mem_kv(mem_prompt, wts["norm_mem"], wts["w_mem_k"], wts["w_mem_v"])
    smem_k = jnp.transpose(cache_mem_k, (0, 1, 3, 4, 2)).reshape(DEPTH, nseq, MEM_W, MEM_TOKENS)
    smem_v = jnp.transpose(cache_mem_v, (0, 1, 3, 4, 2)).reshape(DEPTH, nseq, MEM_W, MEM_TOKENS)
    cache_kpe_t = jnp.transpose(cache_kpe, (0, 1, 3, 2))

    yp = x_prompt.reshape(nb * seq, D_MODEL)
    ys = _to_time_major(x_sample)
    srows = nseq * dseq
    p_ckv, p_kpe, p_conv, p_ffn = [], [], [], []
    s_ckv, s_kpe, s_conv, s_ffn = [], [], [], []
    for l in range(DEPTH):
        qlat, qpe, kall, ckvt, ckv, kpt, mbc, cnew = _in_proj(
            l, yp, nb, seq, PROMPT_TM, 1, tabs_p, wts,
            mem_kv=(mem_k_all, mem_v_all))
        ma = _prompt_attention(l, qlat, qpe, kall, ckvt, wts)
        yp, fnew = _out_ffn(l, yp, [ma, mbc], nb, seq, FFN_TM, 1, wts)
        p_ckv.append(ckv.reshape(nb, seq, KV_LORA))
        p_kpe.append(kpt)
        p_conv.append(cnew)
        p_ffn.append(fnew)

        c0 = _to_time_major(state_conv[l])[None]
        f0 = _to_time_major(state_ffn_conv[l])[None]
        qlat, qpe, _, _, ckv, kpt, mb, qm, cnew = _in_proj(
            l, ys, 1, srows, srows, nseq, tabs_s, wts, carry0=c0)
        ckv_bt = _from_time_major(ckv, dseq)
        kp_dtb = kpt[0].reshape(MLA_ROPE, dseq, nseq)
        qlat_bt = jnp.transpose(qlat[0].reshape(MLA_HEADS, dseq, nseq, KV_LORA),
                                (2, 1, 0, 3)).reshape(nseq, dseq * MLA_HEADS, KV_LORA)
        qpe_bt = jnp.transpose(qpe[0].reshape(dseq, nseq, MLA_HEADS, MLA_ROPE),
                               (1, 0, 2, 3)).reshape(nseq, dseq * MLA_HEADS, MLA_ROPE)
        cnew16 = jnp.pad(ckv_bt, ((0, 0), (0, NEW_PAD - dseq), (0, 0)))
        pnew16 = jnp.pad(jnp.transpose(kp_dtb, (2, 0, 1)), ((0, 0), (0, 0), (0, NEW_PAD - dseq)))
        o_lat = _sample_attention(l, page_table, qlat_bt, qpe_bt, cnew16, pnew16,
                                  cache_ckv, cache_kpe_t)
        o_lat = _to_time_major(o_lat.reshape(nseq, dseq, MLA_HEADS * KV_LORA))
        ma = _uv_proj(l, o_lat, wts)
        qm_bt = jnp.pad(_from_time_major(qm, dseq), ((0, 0), (0, SUBLANES - dseq), (0, 0)))
        mc = _sample_mem(l, qm_bt, smem_k, smem_v, wts)
        mc = _to_time_major(mc[:, :dseq]).astype(BF16)
        ys, fnew = _out_ffn(l, ys, [ma, mb, mc], 1, srows, srows, nseq, wts, carry0=f0)
        s_ckv.append(ckv_bt)
        s_kpe.append(jnp.transpose(kp_dtb, (2, 1, 0)))
        s_conv.append(_from_time_major(cnew[0], 2))
        s_ffn.append(_from_time_major(fnew[0], 2))

    def mem_out(a):
        return jnp.transpose(a.reshape(DEPTH, nb, MEM_HEADS, HEAD_DIM, MEM_TOKENS), (0, 1, 4, 2, 3))

    return (yp.reshape(nb, seq, D_MODEL), _from_time_major(ys, dseq),
            jnp.stack(p_ckv), jnp.transpose(jnp.stack(p_kpe), (0, 1, 3, 2)),
            mem_out(mem_k_all), mem_out(mem_v_all),
            jnp.stack(p_conv), jnp.stack(p_ffn),
            jnp.stack(s_ckv), jnp.stack(s_kpe), jnp.stack(s_conv), jnp.stack(s_ffn))
```

```python
import functools

import jax
import jax.numpy as jnp
import numpy as np
from jax import lax
from jax.experimental import pallas as pl
from jax.experimental.pallas import tpu as pltpu

F32 = jnp.float32
BF16 = jnp.bfloat16

D_MODEL = 1024
DEPTH = 4
HEAD_DIM = 64
MLA_HEADS = 8
MLA_NOPE = 64
MLA_ROPE = 32
MLA_V = 64
MLA_W = MLA_HEADS * MLA_V
KV_LORA = 256
Q_LORA = 384
CONV_W = 256
MEM_HEADS = 4
MEM_W = MEM_HEADS * HEAD_DIM
MEM_TOKENS = 256
D_FF = 2816
ROPE_THETA = 10000.0
EPS = 1e-6
MLA_SCALE = (MLA_NOPE + MLA_ROPE) ** -0.5
MEM_SCALE = HEAD_DIM ** -0.5
NEG = -1e30
LOG2E = 1.4426950408889634

Z_CQ = 0
Z_CKV = Z_CQ + Q_LORA
Z_GB = Z_CKV + KV_LORA
Z_GC = Z_GB + CONV_W
Z_HV = Z_GC + CONV_W
Z_QM = Z_HV + CONV_W
Z_KPE = Z_QM + MEM_W
Z_COLS = Z_KPE + 128

SUBLANES = 8
VMEM_LIMIT_BYTES = 56 * 1024 * 1024

PROMPT_TM = 256
ATTN_TQ = 512
ATTN_TK = 512
PAGE_CHUNK = 2048
NEW_PAD = 16
IN_TM = 512
IN_SUBTILES = 2
FF_CHUNK = 256
FFN_SUBTILES = 2
FFN_TM = 512
SMEM_BLOCK = 8
DMA_UNROLL = 64


def _rms(x, g):
    return x * lax.rsqrt(jnp.mean(x * x, axis=-1, keepdims=True) + EPS) * g


def _dot(a, b):
    return jnp.dot(a, b, preferred_element_type=F32)


def _dot_nt(a, b):
    return lax.dot_general(a, b, (((1,), (1,)), ((), ())), preferred_element_type=F32)


def _resident(shape, index_map):
    return pl.BlockSpec(shape, index_map, pipeline_mode=pl.Buffered(1))


def _params(*sem):
    return pltpu.CompilerParams(dimension_semantics=sem, vmem_limit_bytes=VMEM_LIMIT_BYTES)


def _head_mask(lane, h, width):
    return (lane >= width * h) & (lane < width * (h + 1))


def _carry_base(s):
    return -(-2 * s // SUBLANES) * SUBLANES


def _mem_kv_kernel(mem_ref, g_ref, wk_ref, wv_ref, k_ref, v_ref):
    m = _rms(mem_ref[0], g_ref[0]).astype(BF16)
    k_ref[0, 0] = _dot_nt(wk_ref[0], m)
    v_ref[0, 0] = _dot_nt(wv_ref[0], m)


def _mem_kv(mem, norm_mem, wk, wv):
    nb = mem.shape[0]
    out = jax.ShapeDtypeStruct((DEPTH, nb, MEM_W, MEM_TOKENS), F32)
    return pl.pallas_call(
        _mem_kv_kernel,
        grid=(DEPTH, nb),
        in_specs=[
            pl.BlockSpec((1, MEM_TOKENS, D_MODEL), lambda l, b: (b, 0, 0)),
            pl.BlockSpec((1, 1, D_MODEL), lambda l, b: (l, 0, 0)),
            pl.BlockSpec((1, MEM_W, D_MODEL), lambda l, b: (l, 0, 0)),
            pl.BlockSpec((1, MEM_W, D_MODEL), lambda l, b: (l, 0, 0)),
        ],
        out_specs=[
            pl.BlockSpec((1, 1, MEM_W, MEM_TOKENS), lambda l, b: (l, b, 0, 0)),
            pl.BlockSpec((1, 1, MEM_W, MEM_TOKENS), lambda l, b: (l, b, 0, 0)),
        ],
        out_shape=[out, out],
        compiler_params=_params("parallel", "parallel"),
        name="mem_kv",
    )(mem, norm_mem, wk, wv)


def _mem_scores(qm, mk_t, rows):
    lane = lax.broadcasted_iota(jnp.int32, (rows, MEM_W), 1)
    masks = [_head_mask(lane, h, HEAD_DIM) for h in range(MEM_HEADS)]
    qs = jnp.concatenate([jnp.where(m, qm, 0.0) for m in masks], axis=0).astype(BF16)
    return _dot(qs, mk_t.astype(BF16)) * MEM_SCALE, masks


def _mem_values(sc, masks, mv_t, rows):
    p = jnp.exp(sc - jnp.max(sc, axis=-1, keepdims=True))
    o = _dot_nt(p.astype(BF16), mv_t.astype(BF16)) / jnp.sum(p, axis=-1, keepdims=True)
    om = jnp.where(masks[0], o[0:rows], 0.0)
    for h in range(1, MEM_HEADS):
        om = om + jnp.where(masks[h], o[h * rows:(h + 1) * rows], 0.0)
    return om


def _mem_attend(qm, mk_t, mv_t, rows):
    sc, masks = _mem_scores(qm, mk_t, rows)
    return _mem_values(sc, masks, mv_t, rows)


def _in_proj_kernel(*refs, tm, s, has_carry, do_mem):
    it = iter(refs)
    (x_ref, cosq_ref, sinq_ref, tabk_ref, npm_ref, win_ref, nq_ref, wuq_ref, nkv_ref,
     wuk_ref, cw_ref, ng_ref) = [next(it) for _ in range(12)]
    c0_ref = next(it) if has_carry else None
    if do_mem:
        mk_ref, mv_ref = next(it), next(it)
    qlat_ref, qpe_ref, kall_ref, ckvt_ref, ckv_ref, kpe_ref = [next(it) for _ in range(6)]
    if do_mem:
        mbc_ref = next(it)
    else:
        mb_ref, qm_ref = next(it), next(it)
    cnew_ref = next(it)
    ubuf = next(it)

    i = pl.program_id(1)
    base = _carry_base(s)
    rs = tm // IN_SUBTILES

    @pl.when(i == 0)
    def _():
        if has_carry:
            ubuf[base - 2 * s:base, :] = c0_ref[0]
        else:
            ubuf[0:base, :] = jnp.zeros((base, CONV_W), F32)

    ng = ng_ref[0]
    w = cw_ref[0]

    def project(t):
        rows = slice(t * rs, (t + 1) * rs)
        h = _rms(x_ref[rows, :], npm_ref[0]).astype(BF16)
        return _dot(h, win_ref[0])

    def mix(t, z):
        rows = slice(t * rs, (t + 1) * rs)
        cqn = _rms(z[:, Z_CQ:Z_CKV], nq_ref[0]).astype(BF16)
        q = _dot(cqn, wuq_ref[0])
        qpe = q[:, 512:768] * cosq_ref[rows, :] + q[:, 768:1024] * sinq_ref[rows, :]
        qpe_ref[0, rows, :] = qpe.astype(BF16)
        for j in range(MLA_HEADS // 2):
            ql = _dot(q[:, 128 * j:128 * (j + 1)].astype(BF16), wuk_ref[0, j])
            qlat_ref[0, 2 * j, rows, :] = ql[:, :KV_LORA].astype(BF16)
            qlat_ref[0, 2 * j + 1, rows, :] = ql[:, KV_LORA:].astype(BF16)

        ckv = _rms(z[:, Z_CKV:Z_GB], nkv_ref[0])
        ckv_ref[rows, :] = ckv
        ckvt_ref[0, :, rows] = ckv.T.astype(BF16)
        tk = z[:, Z_KPE:Z_COLS] * tabk_ref[rows, :]
        r = tk + pltpu.roll(tk, 96, 1)
        lane = lax.broadcasted_iota(jnp.int32, (rs, 128), 1)
        kp = jnp.where(lane < MLA_ROPE, r, 0.0)
        kpe_ref[0, :, rows] = kp.T[0:MLA_ROPE, :]
        kp = kp + pltpu.roll(kp, 32, 1)
        kp = (kp + pltpu.roll(kp, 64, 1)).astype(BF16)
        kall_ref[0, rows, 0:KV_LORA] = ckv.astype(BF16)
        kall_ref[0, rows, KV_LORA:KV_LORA + 128] = kp
        kall_ref[0, rows, KV_LORA + 128:KV_LORA + 256] = kp

        u = z[:, Z_GC:Z_HV] * z[:, Z_HV:Z_QM]
        r0 = base + t * rs
        ubuf[r0:r0 + rs, :] = u
        cv = (w[0:1] * ubuf[r0 - 2 * s:r0 - 2 * s + rs, :]
              + w[1:2] * ubuf[r0 - s:r0 - s + rs, :] + w[2:3] * u)
        mb = _rms(z[:, Z_GB:Z_GC] * cv, ng[:, MLA_W:MLA_W + CONV_W]).astype(BF16)

        qm = z[:, Z_QM:Z_KPE]
        if do_mem:
            mbc_ref[rows, 0:CONV_W] = mb
            return _mem_scores(qm, mk_ref[0, 0], rs)
        mb_ref[rows, :] = mb
        qm_ref[rows, :] = qm
        return None

    def mem_finish(t, st):
        if do_mem:
            rows = slice(t * rs, (t + 1) * rs)
            om = _mem_values(st[0], st[1], mv_ref[0, 0], rs)
            mbc_ref[rows, CONV_W:] = _rms(om, ng[:, MLA_W + CONV_W:]).astype(BF16)

    z = project(0)
    for t in range(IN_SUBTILES):
        st = mix(t, z)
        if t + 1 < IN_SUBTILES:
            z = project(t + 1)
        mem_finish(t, st)
    newc = ubuf[base + tm - 2 * s:base + tm, :]
    ubuf[base - 2 * s:base, :] = newc
    cnew_ref[0] = newc


def _in_proj(l, x, nb, t_len, tm, s, tabs, wts, carry0=None, mem_kv=None):
    nt = t_len // tm
    rows = nb * t_len
    has_carry = carry0 is not None
    do_mem = mem_kv is not None
    row_map = lambda b, i: (b * nt + i, 0)
    tab_map = lambda b, i: (i, 0)
    lay = lambda b, i: (l, 0, 0)
    in_specs = [
        pl.BlockSpec((tm, D_MODEL), row_map),
        pl.BlockSpec((tm, 256), tab_map),
        pl.BlockSpec((tm, 256), tab_map),
        pl.BlockSpec((tm, 128), tab_map),
        pl.BlockSpec((1, 1, D_MODEL), lay),
        _resident((1, D_MODEL, Z_COLS), lay),
        pl.BlockSpec((1, 1, Q_LORA), lay),
        _resident((1, Q_LORA, 1024), lay),
        pl.BlockSpec((1, 1, KV_LORA), lay),
        _resident((1, MLA_HEADS // 2, 128, 2 * KV_LORA), lambda b, i: (l, 0, 0, 0)),
        pl.BlockSpec((1, 3, CONV_W), lay),
        pl.BlockSpec((1, 1, D_MODEL), lay),
    ]
    args = [x, *tabs, wts["norm_pre_mix"], wts["w_in"], wts["norm_q"], wts["w_uq"],
            wts["norm_kv"], wts["w_uk2"], wts["conv_w"], wts["norm_group"]]
    if has_carry:
        in_specs.append(pl.BlockSpec((1, 2 * s, CONV_W), lambda b, i: (b, 0, 0)))
        args.append(carry0)
    if do_mem:
        mspec = pl.BlockSpec((1, 1, MEM_W, MEM_TOKENS), lambda b, i: (l, b, 0, 0))
        in_specs += [mspec, mspec]
        args += list(mem_kv)
    out_shape = [
        jax.ShapeDtypeStruct((nb, MLA_HEADS, t_len, KV_LORA), BF16),
        jax.ShapeDtypeStruct((nb, t_len, 256), BF16),
        jax.ShapeDtypeStruct((nb, t_len, 512), BF16),
        jax.ShapeDtypeStruct((nb, KV_LORA, t_len), BF16),
        jax.ShapeDtypeStruct((rows, KV_LORA), F32),
        jax.ShapeDtypeStruct((nb, MLA_ROPE, t_len), F32),
    ]
    out_specs = [
        pl.BlockSpec((1, MLA_HEADS, tm, KV_LORA), lambda b, i: (b, 0, i, 0)),
        pl.BlockSpec((1, tm, 256), lambda b, i: (b, i, 0)),
        pl.BlockSpec((1, tm, 512), lambda b, i: (b, i, 0)),
        pl.BlockSpec((1, KV_LORA, tm), lambda b, i: (b, 0, i)),
        pl.BlockSpec((tm, KV_LORA), row_map),
        pl.BlockSpec((1, MLA_ROPE, tm), lambda b, i: (b, 0, i)),
    ]
    if do_mem:
        out_shape.append(jax.ShapeDtypeStruct((rows, 512), BF16))
        out_specs.append(pl.BlockSpec((tm, 512), row_map))
    else:
        out_shape += [jax.ShapeDtypeStruct((rows, CONV_W), BF16),
                      jax.ShapeDtypeStruct((rows, MEM_W), F32)]
        out_specs += [pl.BlockSpec((tm, CONV_W), row_map), pl.BlockSpec((tm, MEM_W), row_map)]
    out_shape.append(jax.ShapeDtypeStruct((nb, 2 * s, CONV_W), F32))
    out_specs.append(pl.BlockSpec((1, 2 * s, CONV_W), lambda b, i: (b, 0, 0)))
    kern = functools.partial(_in_proj_kernel, tm=tm, s=s, has_carry=has_carry, do_mem=do_mem)
    return pl.pallas_call(
        kern,
        grid=(nb, nt),
        in_specs=in_specs,
        out_specs=out_specs,
        out_shape=out_shape,
        scratch_shapes=[pltpu.VMEM((_carry_base(s) + tm, CONV_W), F32)],
        compiler_params=_params("parallel", "arbitrary"),
        name="in_proj_mem" if do_mem else "in_proj",
    )(*args)


def _causal_pairs(nq, tq, tk):
    it, kt = [], []
    for i in range(nq):
        for k in range((i * tq + tq - 1) // tk + 1):
            it.append(i)
            kt.append(k)
    return np.asarray(it, np.int32), np.asarray(kt, np.int32)


def _attn_kernel(it_ref, kt_ref, qlat_ref, qpe_ref, kall_ref, ckvt_ref, wuvt_ref, ngb_ref, o_ref,
                 q2, m_scr, l_scr, acct, *, tq, tk):
    j = pl.program_id(1)
    i = it_ref[j]
    k = kt_ref[j]
    last_k = (i * tq + tq - 1) // tk
    cols = MLA_HEADS * tq
    c_exp = MLA_SCALE * LOG2E

    @pl.when(k == 0)
    def _():
        lane = lax.broadcasted_iota(jnp.int32, (tq, 256), 1)
        qp = qpe_ref[0]
        zero = jnp.zeros_like(qp)
        for h in range(MLA_HEADS):
            q2[h * tq:(h + 1) * tq, 0:KV_LORA] = qlat_ref[0, h]
            q2[h * tq:(h + 1) * tq, KV_LORA:] = jnp.where(_head_mask(lane, h, MLA_ROPE), qp, zero)
        m_scr[...] = jnp.full((1, cols), NEG, F32)
        l_scr[...] = jnp.zeros((1, cols), F32)
        acct[...] = jnp.zeros((KV_LORA, cols), F32)

    def scores():
        return _dot_nt(kall_ref[0], q2[...])

    def step(sc):
        m_prev = m_scr[...]
        m_new = jnp.maximum(m_prev, jnp.max(sc, axis=0, keepdims=True))
        alpha = jnp.exp2((m_prev - m_new) * c_exp)
        p = jnp.exp2((sc - m_new) * c_exp)
        l_scr[...] = alpha * l_scr[...] + jnp.sum(p, axis=0, keepdims=True)
        acct[...] = alpha * acct[...] + _dot(ckvt_ref[0], p.astype(BF16))
        m_scr[...] = m_new

    @pl.when(k < last_k)
    def _():
        step(scores())

    @pl.when(k == last_k)
    def _():
        k_pos = k * tk + lax.broadcasted_iota(jnp.int32, (tk, cols), 0)
        q_pos = i * tq + (lax.broadcasted_iota(jnp.int32, (tk, cols), 1) & (tq - 1))
        step(jnp.where(k_pos <= q_pos, scores(), NEG))
        ot = (acct[...] * (1.0 / l_scr[...])).astype(BF16)
        omt = jnp.concatenate(
            [_dot(wuvt_ref[0, h], ot[:, h * tq:(h + 1) * tq]) for h in range(MLA_HEADS)], axis=0)
        y = omt * lax.rsqrt(jnp.mean(omt * omt, axis=0, keepdims=True) + EPS) * ngb_ref[0]
        o_ref[...] = y.T.astype(BF16)


def _prompt_attention(l, qlat, qpe, kall, ckvt, wts):
    nb, _, t_len, _ = qlat.shape
    tq, tk = ATTN_TQ, ATTN_TK
    nq = t_len // tq
    cols = MLA_HEADS * tq
    it, kt = _causal_pairs(nq, tq, tk)
    kern = functools.partial(_attn_kernel, tq=tq, tk=tk)
    grid_spec = pltpu.PrefetchScalarGridSpec(
        num_scalar_prefetch=2,
        grid=(nb, it.shape[0]),
        in_specs=[
            pl.BlockSpec((1, MLA_HEADS, tq, KV_LORA), lambda b, j, it, kt: (b, 0, it[j], 0)),
            pl.BlockSpec((1, tq, 256), lambda b, j, it, kt: (b, it[j], 0)),
            pl.BlockSpec((1, tk, 512), lambda b, j, it, kt: (b, kt[j], 0)),
            pl.BlockSpec((1, KV_LORA, tk), lambda b, j, it, kt: (b, 0, kt[j])),
            _resident((1, MLA_HEADS, MLA_V, KV_LORA), lambda b, j, it, kt: (l, 0, 0, 0)),
            pl.BlockSpec((1, MLA_W, tq), lambda b, j, it, kt: (l, 0, 0)),
        ],
        out_specs=pl.BlockSpec((tq, MLA_W), lambda b, j, it, kt: (b * nq + it[j], 0)),
        scratch_shapes=[
            pltpu.VMEM((cols, 512), BF16),
            pltpu.VMEM((1, cols), F32),
            pltpu.VMEM((1, cols), F32),
            pltpu.VMEM((KV_LORA, cols), F32),
        ],
    )
    return pl.pallas_call(
        kern,
        grid_spec=grid_spec,
        out_shape=jax.ShapeDtypeStruct((nb * t_len, MLA_W), BF16),
        compiler_params=_params("parallel", "arbitrary"),
        name="prompt_attn",
    )(jnp.asarray(it), jnp.asarray(kt), qlat, qpe, kall, ckvt, wts["w_uv_t"], wts["ng_attn"])


def _sample_attn_kernel(pt_ref, qlat_ref, qpe_ref, cnew_ref, pnew_ref, ckv_hbm, kpe_hbm, o_ref,
                        kbuf, pbuf, kb16, s_scr, sem, *, l, n_pages, page):
    b = pl.program_id(0)
    nb = pl.num_programs(0)
    slot = b % 2
    past = n_pages * page

    def page_copies(seq, sl, p):
        pg = pt_ref[seq, p]
        dst = pl.ds(pl.multiple_of(p * page, page), page)
        return (pltpu.make_async_copy(ckv_hbm.at[l, pg], kbuf.at[sl, dst, :], sem.at[0, sl]),
                pltpu.make_async_copy(kpe_hbm.at[l, pg], pbuf.at[sl, :, dst], sem.at[1, sl]))

    def issue(seq, sl):
        def body(p, c):
            for cp in page_copies(seq, sl, p):
                cp.start()
            return c
        lax.fori_loop(0, n_pages, body, 0, unroll=min(DMA_UNROLL, n_pages))

    @pl.when(b == 0)
    def _():
        issue(0, 0)

    @pl.when(b + 1 < nb)
    def _():
        issue(b + 1, 1 - slot)

    first = pl.ds(0, page)
    for _ in range(n_pages):
        pltpu.make_async_copy(ckv_hbm.at[l, 0], kbuf.at[slot, first, :], sem.at[0, slot]).wait()
    for _ in range(n_pages):
        pltpu.make_async_copy(kpe_hbm.at[l, 0], pbuf.at[slot, :, first], sem.at[1, slot]).wait()

    q = qlat_ref[0]
    qp = qpe_ref[0]
    nrow = q.shape[0]
    nch = past // PAGE_CHUNK
    for c in range(nch):
        sl_c = slice(c * PAGE_CHUNK, (c + 1) * PAGE_CHUNK)
        kb16[sl_c, :] = kbuf[slot, sl_c, :].astype(BF16)
    s_scr[...] = (_dot_nt(q, kb16[...]) + _dot(qp, pbuf[slot].astype(BF16))) * MLA_SCALE

    @pl.when(b >= 0)
    def _():
        cn = cnew_ref[0].astype(BF16)
        pn = pnew_ref[0].astype(BF16)
        s_new = (_dot_nt(q, cn) + _dot(qp, pn)) * MLA_SCALE
        t_row = lax.broadcasted_iota(jnp.int32, (nrow, NEW_PAD), 0) // MLA_HEADS
        j_col = lax.broadcasted_iota(jnp.int32, (nrow, NEW_PAD), 1)
        s_new = jnp.where(j_col <= t_row, s_new, NEG)
        s_all = s_scr[...]
        m = jnp.maximum(jnp.max(s_all, axis=-1, keepdims=True),
                        jnp.max(s_new, axis=-1, keepdims=True))
        p_new = jnp.exp(s_new - m)
        p_all = jnp.exp(s_all - m)
        denom = jnp.sum(p_all, axis=-1, keepdims=True) + jnp.sum(p_new, axis=-1, keepdims=True)
        s_scr[...] = p_all
        accs = [_dot(p_new.astype(BF16), cn), None]
        for c in range(nch):
            sl_c = slice(c * PAGE_CHUNK, (c + 1) * PAGE_CHUNK)
            d = _dot(s_scr[:, sl_c].astype(BF16), kb16[sl_c, :])
            accs[c % 2] = d if accs[c % 2] is None else accs[c % 2] + d
        acc = accs[0] if accs[1] is None else accs[0] + accs[1]
        o_ref[0] = acc / denom


def _sample_attention(l, page_table, qlat, qpe, cnew, pnew_t, cache_ckv, cache_kpe_t):
    nseq, nrow, _ = qlat.shape
    n_pages = page_table.shape[1]
    page = cache_ckv.shape[2]
    past = n_pages * page
    kern = functools.partial(_sample_attn_kernel, l=l, n_pages=n_pages, page=page)
    grid_spec = pltpu.PrefetchScalarGridSpec(
        num_scalar_prefetch=1,
        grid=(nseq,),
        in_specs=[
            pl.BlockSpec((1, nrow, KV_LORA), lambda b, pt: (b, 0, 0)),
            pl.BlockSpec((1, nrow, MLA_ROPE), lambda b, pt: (b, 0, 0)),
            pl.BlockSpec((1, NEW_PAD, KV_LORA), lambda b, pt: (b, 0, 0)),
            pl.BlockSpec((1, MLA_ROPE, NEW_PAD), lambda b, pt: (b, 0, 0)),
            pl.BlockSpec(memory_space=pl.ANY),
            pl.BlockSpec(memory_space=pl.ANY),
        ],
        out_specs=pl.BlockSpec((1, nrow, KV_LORA), lambda b, pt: (b, 0, 0)),
        scratch_shapes=[
            pltpu.VMEM((2, past, KV_LORA), F32),
            pltpu.VMEM((2, MLA_ROPE, past), F32),
            pltpu.VMEM((past, KV_LORA), BF16),
            pltpu.VMEM((nrow, past), F32),
            pltpu.SemaphoreType.DMA((2, 2)),
        ],
    )
    return pl.pallas_call(
        kern,
        grid_spec=grid_spec,
        out_shape=jax.ShapeDtypeStruct((nseq, nrow, KV_LORA), F32),
        compiler_params=_params("arbitrary"),
        name="sample_attn",
    )(page_table, qlat, qpe, cnew, pnew_t, cache_ckv, cache_kpe_t)


def _uv_kernel(o_ref, w_ref, ng_ref, out_ref):
    om = _dot(o_ref[...].astype(BF16), w_ref[0])
    out_ref[...] = _rms(om, ng_ref[0][:, 0:MLA_W]).astype(BF16)


def _uv_proj(l, o_lat, wts):
    rows = o_lat.shape[0]
    return pl.pallas_call(
        _uv_kernel,
        grid=(1,),
        in_specs=[
            pl.BlockSpec((rows, MLA_HEADS * KV_LORA), lambda i: (0, 0)),
            pl.BlockSpec((1, MLA_HEADS * KV_LORA, MLA_W), lambda i: (l, 0, 0)),
            pl.BlockSpec((1, 1, D_MODEL), lambda i: (l, 0, 0)),
        ],
        out_specs=pl.BlockSpec((rows, MLA_W), lambda i: (0, 0)),
        out_shape=jax.ShapeDtypeStruct((rows, MLA_W), BF16),
        compiler_params=_params("arbitrary"),
        name="uv_proj",
    )(o_lat, wts["w_uv_flat"], wts["norm_group"])


def _sample_mem_kernel(q_ref, mk_ref, mv_ref, ng_ref, o_ref):
    ng = ng_ref[0]
    for j in range(SMEM_BLOCK):
        om = _mem_attend(q_ref[j], mk_ref[0, j], mv_ref[0, j], SUBLANES)
        o_ref[j] = _rms(om, ng[:, MLA_W + CONV_W:])


def _sample_mem(l, q, mem_k, mem_v, wts):
    nseq = q.shape[0]
    mspec = pl.BlockSpec((1, SMEM_BLOCK, MEM_W, MEM_TOKENS), lambda j: (l, j, 0, 0))
    return pl.pallas_call(
        _sample_mem_kernel,
        grid=(nseq // SMEM_BLOCK,),
        in_specs=[
            pl.BlockSpec((SMEM_BLOCK, SUBLANES, MEM_W), lambda j: (j, 0, 0)),
            mspec, mspec,
            pl.BlockSpec((1, 1, D_MODEL), lambda j: (l, 0, 0)),
        ],
        out_specs=pl.BlockSpec((SMEM_BLOCK, SUBLANES, MEM_W), lambda j: (j, 0, 0)),
        out_shape=jax.ShapeDtypeStruct((nseq, SUBLANES, MEM_W), F32),
        compiler_params=_params("parallel"),
        name="sample_mem",
    )(q, mem_k, mem_v, wts["norm_group"])


def _ffn_kernel(*refs, tm, s, has_carry, widths):
    it = iter(refs)
    x_ref = next(it)
    piece_refs = [next(it) for _ in widths]
    (wout_ref, npost_ref, npre_ref, wg_ref, wu_ref, fcw_ref, fcb_ref, wd_ref,
     npf_ref) = [next(it) for _ in range(9)]
    c0_ref = next(it) if has_carry else None
    y_ref, fnew_ref, gbuf = next(it), next(it), next(it)

    i = pl.program_id(1)
    base = _carry_base(s)
    rs = tm // FFN_SUBTILES
    nchunk = D_FF // FF_CHUNK
    cols = lambda c: slice(c * FF_CHUNK, (c + 1) * FF_CHUNK)

    @pl.when(i == 0)
    def _():
        if has_carry:
            gbuf[base - 2 * s:base, :] = c0_ref[0]
        else:
            gbuf[0:base, :] = jnp.zeros((base, D_FF), F32)

    w = fcw_ref[0]
    bias = fcb_ref[0]

    def prologue(t):
        rows = slice(t * rs, (t + 1) * rs)
        att = None
        off = 0
        for pref, wd in zip(piece_refs, widths):
            part = _dot(pref[rows, :], wout_ref[0, off:off + wd, :])
            att = part if att is None else att + part
            off += wd
        x1 = x_ref[rows, :] + _rms(att, npost_ref[0])
        return x1, _rms(x1, npre_ref[0]).astype(BF16)

    def gate_up(h2, c):
        return _dot(h2, wg_ref[0, :, cols(c)]), _dot(h2, wu_ref[0, :, cols(c)])

    def act(t, c, g, up):
        r0 = base + t * rs
        cs = cols(c)
        gbuf[r0:r0 + rs, cs] = g
        gc = (w[0:1, cs] * gbuf[r0 - 2 * s:r0 - 2 * s + rs, cs]
              + w[1:2, cs] * gbuf[r0 - s:r0 - s + rs, cs] + w[2:3, cs] * g + bias[:, cs])
        return (gc * (1.0 / (1.0 + jnp.exp(-gc))) * up).astype(BF16)

    x1s, h2s = [None] * FFN_SUBTILES, [None] * FFN_SUBTILES
    x1s[0], h2s[0] = prologue(0)
    items = [(t, c) for t in range(FFN_SUBTILES) for c in range(nchunk)]
    nxt = gate_up(h2s[0], 0)
    dn = None
    for idx, (t, c) in enumerate(items):
        g, up = nxt
        if idx == 0:
            for t2 in range(1, FFN_SUBTILES):
                x1s[t2], h2s[t2] = prologue(t2)
        if idx + 1 < len(items):
            t_n, c_n = items[idx + 1]
            nxt = gate_up(h2s[t_n], c_n)
        part = _dot(act(t, c, g, up), wd_ref[0, cols(c), :])
        dn = part if c == 0 else dn + part
        if c == nchunk - 1:
            y_ref[t * rs:(t + 1) * rs, :] = x1s[t] + _rms(dn, npf_ref[0])
    newc = gbuf[base + tm - 2 * s:base + tm, :]
    gbuf[base - 2 * s:base, :] = newc
    fnew_ref[0] = newc


def _out_ffn(l, x, pieces, nb, t_len, tm, s, wts, carry0=None):
    nt = t_len // tm
    rows = nb * t_len
    has_carry = carry0 is not None
    widths = tuple(p.shape[1] for p in pieces)
    row_map = lambda b, i: (b * nt + i, 0)
    lay = lambda b, i: (l, 0, 0)
    in_specs = [pl.BlockSpec((tm, D_MODEL), row_map)]
    in_specs += [pl.BlockSpec((tm, wd), row_map) for wd in widths]
    in_specs += [
        _resident((1, D_MODEL, D_MODEL), lay),
        pl.BlockSpec((1, 1, D_MODEL), lay),
        pl.BlockSpec((1, 1, D_MODEL), lay),
        _resident((1, D_MODEL, D_FF), lay),
        _resident((1, D_MODEL, D_FF), lay),
        pl.BlockSpec((1, 3, D_FF), lay),
        pl.BlockSpec((1, 1, D_FF), lay),
        _resident((1, D_FF, D_MODEL), lay),
        pl.BlockSpec((1, 1, D_MODEL), lay),
    ]
    args = [x, *pieces, wts["w_out"], wts["norm_post_mix"], wts["norm_pre_ffn"], wts["w_gate"],
            wts["w_up"], wts["ffn_conv_w"], wts["ffn_conv_b"], wts["w_down"],
            wts["norm_post_ffn"]]
    if has_carry:
        in_specs.append(pl.BlockSpec((1, 2 * s, D_FF), lambda b, i: (b, 0, 0)))
        args.append(carry0)
    kern = functools.partial(_ffn_kernel, tm=tm, s=s, has_carry=has_carry, widths=widths)
    return pl.pallas_call(
        kern,
        grid=(nb, nt),
        in_specs=in_specs,
        out_specs=[pl.BlockSpec((tm, D_MODEL), row_map),
                   pl.BlockSpec((1, 2 * s, D_FF), lambda b, i: (b, 0, 0))],
        out_shape=[jax.ShapeDtypeStruct((rows, D_MODEL), F32),
                   jax.ShapeDtypeStruct((nb, 2 * s, D_FF), F32)],
        scratch_shapes=[pltpu.VMEM((_carry_base(s) + tm, D_FF), F32)],
        compiler_params=_params("parallel", "arbitrary"),
        name="out_ffn",
    )(*args)


def _rot_pair(w):
    half = MLA_ROPE // 2
    return jnp.concatenate([-w[..., half:], w[..., :half]], axis=-1)


def _prep_weights(norm_pre_mix, w_in, norm_q, w_uq, norm_kv, w_uk, w_uv, conv_w, norm_mem,
                  w_mem_k, w_mem_v, norm_group, w_out, norm_post_mix, norm_pre_ffn, w_gate, w_up,
                  ffn_conv_w, ffn_conv_b, w_down, norm_post_ffn):
    o0 = Q_LORA
    o1 = o0 + KV_LORA
    o2 = o1 + MLA_ROPE
    o3 = o2 + CONV_W
    o4 = o3 + CONV_W
    o5 = o4 + CONV_W
    kpe_w = w_in[..., o1:o2]
    w_in_r = jnp.concatenate(
        [w_in[..., :o1], w_in[..., o2:], kpe_w, _rot_pair(kpe_w),
         jnp.zeros(w_in.shape[:2] + (128 - 2 * MLA_ROPE,), w_in.dtype)], axis=-1).astype(BF16)

    per_head = w_uq.reshape(DEPTH, Q_LORA, MLA_HEADS, MLA_NOPE + MLA_ROPE)
    nope = per_head[..., :MLA_NOPE].reshape(DEPTH, Q_LORA, MLA_HEADS * MLA_NOPE)
    pe = per_head[..., MLA_NOPE:]
    w_uq_r = jnp.concatenate(
        [nope, pe.reshape(DEPTH, Q_LORA, -1), _rot_pair(pe).reshape(DEPTH, Q_LORA, -1)],
        axis=-1).astype(BF16)

    wt = jnp.transpose(w_uk, (0, 2, 3, 1)).reshape(DEPTH, MLA_HEADS // 2, 2, MLA_NOPE, KV_LORA)
    w_uk2 = jnp.einsum("ljanr,ab->ljanbr", wt, jnp.eye(2, dtype=wt.dtype)).reshape(
        DEPTH, MLA_HEADS // 2, 2 * MLA_NOPE, 2 * KV_LORA).astype(BF16)

    w_uv_exp = jnp.einsum("lrhv,hg->lhrgv", w_uv, jnp.eye(MLA_HEADS, dtype=w_uv.dtype)).reshape(
        DEPTH, MLA_HEADS, KV_LORA, MLA_W).astype(BF16)

    vec = lambda a: a.reshape(DEPTH, 1, a.shape[-1])
    return dict(
        norm_pre_mix=vec(norm_pre_mix), w_in=w_in_r, norm_q=vec(norm_q), w_uq=w_uq_r,
        norm_kv=vec(norm_kv), w_uk2=w_uk2,
        w_uv_t=jnp.transpose(w_uv, (0, 2, 3, 1)).astype(BF16),
        ng_attn=jnp.broadcast_to(norm_group[:, :MLA_W, None], (DEPTH, MLA_W, ATTN_TQ)),
        w_uv_flat=w_uv_exp.reshape(DEPTH, MLA_HEADS * KV_LORA, MLA_W), conv_w=conv_w,
        norm_mem=vec(norm_mem), w_mem_k=jnp.transpose(w_mem_k, (0, 2, 1)).astype(BF16),
        w_mem_v=jnp.transpose(w_mem_v, (0, 2, 1)).astype(BF16),
        norm_group=vec(norm_group), w_out=w_out.astype(BF16), norm_post_mix=vec(norm_post_mix),
        norm_pre_ffn=vec(norm_pre_ffn), w_gate=w_gate.astype(BF16), w_up=w_up.astype(BF16),
        ffn_conv_w=ffn_conv_w, ffn_conv_b=vec(ffn_conv_b), w_down=w_down.astype(BF16),
        norm_post_ffn=vec(norm_post_ffn))


def _rope_tables(pos):
    half = MLA_ROPE // 2
    freqs = ROPE_THETA ** (-jnp.arange(half, dtype=F32) * (2.0 / MLA_ROPE))
    ang = pos.astype(F32)[:, None] * freqs[None, :]
    cos = jnp.cos(ang)
    sin = jnp.sin(ang)
    cos32 = jnp.concatenate([cos, cos], axis=-1)
    sin32 = jnp.concatenate([sin, sin], axis=-1)
    tabk = jnp.concatenate([cos32, sin32, jnp.zeros((pos.shape[0], 64), F32)], axis=-1)
    return jnp.tile(cos32, (1, MLA_HEADS)), jnp.tile(sin32, (1, MLA_HEADS)), tabk


def _to_time_major(a):
    return jnp.transpose(a, (1, 0, 2)).reshape(-1, a.shape[-1])


def _from_time_major(a, t_len):
    return jnp.transpose(a.reshape(t_len, -1, a.shape[-1]), (1, 0, 2))


def kernel(x_prompt, x_sample, mem_prompt, cache_ckv, cache_kpe, cache_mem_k, cache_mem_v, state_conv, state_ffn_conv, page_table, norm_pre_mix, w_in, norm_q, w_uq, norm_kv, w_uk, w_uv, conv_w, norm_mem, w_mem_k, w_mem_v, norm_group, w_out, norm_post_mix, norm_pre_ffn, w_gate, w_up, ffn_conv_w, ffn_conv_b, w_down, norm_post_ffn):
    nb, seq, _ = x_prompt.shape
    nseq, dseq, _ = x_sample.shape
    n_pages = page_table.shape[1]
    page = cache_ckv.shape[2]
    past = n_pages * page
    wts = _prep_weights(norm_pre_mix, w_in, norm_q, w_uq, norm_kv, w_uk, w_uv, conv_w, norm_mem,
                        w_mem_k, w_mem_v, norm_group, w_out, norm_post_mix, norm_pre_ffn, w_gate,
                        w_up, ffn_conv_w, ffn_conv_b, w_down, norm_post_ffn)

    tabs_p = _rope_tables(jnp.arange(seq, dtype=jnp.int32))
    pos_s = jnp.repeat(past + jnp.arange(dseq, dtype=jnp.int32), nseq)
    tabs_s = _rope_tables(pos_s)

    mem_k_all, mem_v_all = _mem_kv(mem_prompt, wts["norm_mem"], wts["w_mem_k"], wts["w_mem_v"])
    smem_k = jnp.transpose(cache_mem_k, (0, 1, 3, 4, 2)).reshape(DEPTH, nseq, MEM_W, MEM_TOKENS)
    smem_v = jnp.transpose(cache_mem_v, (0, 1, 3, 4, 2)).reshape(DEPTH, nseq, MEM_W, MEM_TOKENS)
    cache_kpe_t = jnp.transpose(cache_kpe, (0, 1, 3, 2))

    yp = x_prompt.reshape(nb * seq, D_MODEL)
    ys = _to_time_major(x_sample)
    srows = nseq * dseq
    p_ckv, p_kpe, p_conv, p_ffn = [], [], [], []
    s_ckv, s_kpe, s_conv, s_ffn = [], [], [], []
    for l in range(DEPTH):
        qlat, qpe, kall, ckvt, ckv, kpt, mbc, cnew = _in_proj(
            l, yp, nb, seq, IN_TM, 1, tabs_p, wts,
            mem_kv=(mem_k_all, mem_v_all))
        ma = _prompt_attention(l, qlat, qpe, kall, ckvt, wts)
        yp, fnew = _out_ffn(l, yp, [ma, mbc], nb, seq, FFN_TM, 1, wts)
        p_ckv.append(ckv.reshape(nb, seq, KV_LORA))
        p_kpe.append(kpt)
        p_conv.append(cnew)
        p_ffn.append(fnew)

        c0 = _to_time_major(state_conv[l])[None]
        f0 = _to_time_major(state_ffn_conv[l])[None]
        qlat, qpe, _, _, ckv, kpt, mb, qm, cnew = _in_proj(
            l, ys, 1, srows, srows, nseq, tabs_s, wts, carry0=c0)
        ckv_bt = _from_time_major(ckv, dseq)
        kp_dtb = kpt[0].reshape(MLA_ROPE, dseq, nseq)
        qlat_bt = jnp.transpose(qlat[0].reshape(MLA_HEADS, dseq, nseq, KV_LORA),
                                (2, 1, 0, 3)).reshape(nseq, dseq * MLA_HEADS, KV_LORA)
        qpe_bt = jnp.transpose(qpe[0].reshape(dseq, nseq, MLA_HEADS, MLA_ROPE),
                               (1, 0, 2, 3)).reshape(nseq, dseq * MLA_HEADS, MLA_ROPE)
        cnew16 = jnp.pad(ckv_bt, ((0, 0), (0, NEW_PAD - dseq), (0, 0)))
        pnew16 = jnp.pad(jnp.transpose(kp_dtb, (2, 0, 1)), ((0, 0), (0, 0), (0, NEW_PAD - dseq)))
        o_lat = _sample_attention(l, page_table, qlat_bt, qpe_bt, cnew16, pnew16,
                                  cache_ckv, cache_kpe_t)
        o_lat = _to_time_major(o_lat.reshape(nseq, dseq, MLA_HEADS * KV_LORA))
        ma = _uv_proj(l, o_lat, wts)
        qm_bt = jnp.pad(_from_time_major(qm, dseq), ((0, 0), (0, SUBLANES - dseq), (0, 0)))
        mc = _sample_mem(l, qm_bt, smem_k, smem_v, wts)
        mc = _to_time_major(mc[:, :dseq]).astype(BF16)
        ys, fnew = _out_ffn(l, ys, [ma, mb, mc], 1, srows, srows, nseq, wts, carry0=f0)
        s_ckv.append(ckv_bt)
        s_kpe.append(jnp.transpose(kp_dtb, (2, 1, 0)))
        s_conv.append(_from_time_major(cnew[0], 2))
        s_ffn.append(_from_time_major(fnew[0], 2))

    def mem_out(a):
        return jnp.transpose(a.reshape(DEPTH, nb, MEM_HEADS, HEAD_DIM, MEM_TOKENS), (0, 1, 4, 2, 3))

    return (yp.reshape(nb, seq, D_MODEL), _from_time_major(ys, dseq),
            jnp.stack(p_ckv), jnp.transpose(jnp.stack(p_kpe), (0, 1, 3, 2)),
            mem_out(mem_k_all), mem_out(mem_v_all),
            jnp.stack(p_conv), jnp.stack(p_ffn),
            jnp.stack(s_ckv), jnp.stack(s_kpe), jnp.stack(s_conv), jnp.stack(s_ffn))
```

```python
import functools

import jax
import jax.numpy as jnp
import numpy as np
from jax import lax
from jax.experimental import pallas as pl
from jax.experimental.pallas import tpu as pltpu

F32 = jnp.float32
BF16 = jnp.bfloat16

D_MODEL = 1024
DEPTH = 4
HEAD_DIM = 64
MLA_HEADS = 8
MLA_NOPE = 64
MLA_ROPE = 32
MLA_V = 64
MLA_W = MLA_HEADS * MLA_V
KV_LORA = 256
Q_LORA = 384
CONV_W = 256
MEM_HEADS = 4
MEM_W = MEM_HEADS * HEAD_DIM
MEM_TOKENS = 256
D_FF = 2816
ROPE_THETA = 10000.0
EPS = 1e-6
MLA_SCALE = (MLA_NOPE + MLA_ROPE) ** -0.5
MEM_SCALE = HEAD_DIM ** -0.5
NEG = -1e30
LOG2E = 1.4426950408889634

Z_CQ = 0
Z_CKV = Z_CQ + Q_LORA
Z_GB = Z_CKV + KV_LORA
Z_GC = Z_GB + CONV_W
Z_HV = Z_GC + CONV_W
Z_QM = Z_HV + CONV_W
Z_KPE = Z_QM + MEM_W
Z_COLS = Z_KPE + 128

SUBLANES = 8
VMEM_LIMIT_BYTES = 56 * 1024 * 1024

PROMPT_TM = 256
ATTN_TQ = 512
ATTN_TK = 512
ATTN_HALF = 256
PAGE_CHUNK = 2048
NEW_PAD = 16
IN_TM = 512
IN_SUBTILES = 2
FF_CHUNK = 256
FFN_SUBTILES = 2
FFN_TM = 512
SMEM_BLOCK = 8
DMA_UNROLL = 64


def _rms(x, g):
    return x * lax.rsqrt(jnp.mean(x * x, axis=-1, keepdims=True) + EPS) * g


def _dot(a, b):
    return jnp.dot(a, b, preferred_element_type=F32)


def _dot_nt(a, b):
    return lax.dot_general(a, b, (((1,), (1,)), ((), ())), preferred_element_type=F32)


def _resident(shape, index_map):
    return pl.BlockSpec(shape, index_map, pipeline_mode=pl.Buffered(1))


def _params(*sem):
    return pltpu.CompilerParams(dimension_semantics=sem, vmem_limit_bytes=VMEM_LIMIT_BYTES)


def _head_mask(lane, h, width):
    return (lane >= width * h) & (lane < width * (h + 1))


def _carry_base(s):
    return -(-2 * s // SUBLANES) * SUBLANES


def _mem_kv_kernel(mem_ref, g_ref, wk_ref, wv_ref, k_ref, v_ref):
    m = _rms(mem_ref[0], g_ref[0]).astype(BF16)
    k_ref[0, 0] = _dot_nt(wk_ref[0], m)
    v_ref[0, 0] = _dot_nt(wv_ref[0], m)


def _mem_kv(mem, norm_mem, wk, wv):
    nb = mem.shape[0]
    out = jax.ShapeDtypeStruct((DEPTH, nb, MEM_W, MEM_TOKENS), F32)
    return pl.pallas_call(
        _mem_kv_kernel,
        grid=(DEPTH, nb),
        in_specs=[
            pl.BlockSpec((1, MEM_TOKENS, D_MODEL), lambda l, b: (b, 0, 0)),
            pl.BlockSpec((1, 1, D_MODEL), lambda l, b: (l, 0, 0)),
            pl.BlockSpec((1, MEM_W, D_MODEL), lambda l, b: (l, 0, 0)),
            pl.BlockSpec((1, MEM_W, D_MODEL), lambda l, b: (l, 0, 0)),
        ],
        out_specs=[
            pl.BlockSpec((1, 1, MEM_W, MEM_TOKENS), lambda l, b: (l, b, 0, 0)),
            pl.BlockSpec((1, 1, MEM_W, MEM_TOKENS), lambda l, b: (l, b, 0, 0)),
        ],
        out_shape=[out, out],
        compiler_params=_params("parallel", "parallel"),
        name="mem_kv",
    )(mem, norm_mem, wk, wv)


def _mem_scores(qm, mk_t, rows):
    lane = lax.broadcasted_iota(jnp.int32, (rows, MEM_W), 1)
    masks = [_head_mask(lane, h, HEAD_DIM) for h in range(MEM_HEADS)]
    qs = jnp.concatenate([jnp.where(m, qm, 0.0) for m in masks], axis=0).astype(BF16)
    return _dot(qs, mk_t.astype(BF16)) * MEM_SCALE, masks


def _mem_values(sc, masks, mv_t, rows):
    p = jnp.exp(sc - jnp.max(sc, axis=-1, keepdims=True))
    o = _dot_nt(p.astype(BF16), mv_t.astype(BF16)) / jnp.sum(p, axis=-1, keepdims=True)
    om = jnp.where(masks[0], o[0:rows], 0.0)
    for h in range(1, MEM_HEADS):
        om = om + jnp.where(masks[h], o[h * rows:(h + 1) * rows], 0.0)
    return om


def _mem_attend(qm, mk_t, mv_t, rows):
    sc, masks = _mem_scores(qm, mk_t, rows)
    return _mem_values(sc, masks, mv_t, rows)


def _in_proj_kernel(*refs, tm, s, has_carry, do_mem):
    it = iter(refs)
    (x_ref, cosq_ref, sinq_ref, tabk_ref, npm_ref, win_ref, nq_ref, wuq_ref, nkv_ref,
     wuk_ref, cw_ref, ng_ref) = [next(it) for _ in range(12)]
    c0_ref = next(it) if has_carry else None
    if do_mem:
        mk_ref, mv_ref = next(it), next(it)
    qlat_ref, qpe_ref, kall_ref, ckvt_ref, ckv_ref, kpe_ref = [next(it) for _ in range(6)]
    if do_mem:
        mbc_ref = next(it)
    else:
        mb_ref, qm_ref = next(it), next(it)
    cnew_ref = next(it)
    ubuf = next(it)

    i = pl.program_id(1)
    base = _carry_base(s)
    rs = tm // IN_SUBTILES

    @pl.when(i == 0)
    def _():
        if has_carry:
            ubuf[base - 2 * s:base, :] = c0_ref[0]
        else:
            ubuf[0:base, :] = jnp.zeros((base, CONV_W), F32)

    ng = ng_ref[0]
    w = cw_ref[0]

    def project(t):
        rows = slice(t * rs, (t + 1) * rs)
        h = _rms(x_ref[rows, :], npm_ref[0]).astype(BF16)
        return _dot(h, win_ref[0])

    def mix(t, z):
        rows = slice(t * rs, (t + 1) * rs)
        cqn = _rms(z[:, Z_CQ:Z_CKV], nq_ref[0]).astype(BF16)
        q = _dot(cqn, wuq_ref[0])
        qpe = q[:, 512:768] * cosq_ref[rows, :] + q[:, 768:1024] * sinq_ref[rows, :]
        qpe_ref[0, rows, :] = qpe.astype(BF16)
        for j in range(MLA_HEADS // 2):
            ql = _dot(q[:, 128 * j:128 * (j + 1)].astype(BF16), wuk_ref[0, j])
            qlat_ref[0, 2 * j, rows, :] = ql[:, :KV_LORA].astype(BF16)
            qlat_ref[0, 2 * j + 1, rows, :] = ql[:, KV_LORA:].astype(BF16)

        ckv = _rms(z[:, Z_CKV:Z_GB], nkv_ref[0])
        ckv_ref[rows, :] = ckv
        ckvt_ref[0, :, rows] = ckv.T.astype(BF16)
        tk = z[:, Z_KPE:Z_COLS] * tabk_ref[rows, :]
        r = tk + pltpu.roll(tk, 96, 1)
        lane = lax.broadcasted_iota(jnp.int32, (rs, 128), 1)
        kp = jnp.where(lane < MLA_ROPE, r, 0.0)
        kpe_ref[0, :, rows] = kp.T[0:MLA_ROPE, :]
        kp = kp + pltpu.roll(kp, 32, 1)
        kp = (kp + pltpu.roll(kp, 64, 1)).astype(BF16)
        kall_ref[0, rows, 0:KV_LORA] = ckv.astype(BF16)
        kall_ref[0, rows, KV_LORA:KV_LORA + 128] = kp
        kall_ref[0, rows, KV_LORA + 128:KV_LORA + 256] = kp

        u = z[:, Z_GC:Z_HV] * z[:, Z_HV:Z_QM]
        r0 = base + t * rs
        ubuf[r0:r0 + rs, :] = u
        cv = (w[0:1] * ubuf[r0 - 2 * s:r0 - 2 * s + rs, :]
              + w[1:2] * ubuf[r0 - s:r0 - s + rs, :] + w[2:3] * u)
        mb = _rms(z[:, Z_GB:Z_GC] * cv, ng[:, MLA_W:MLA_W + CONV_W]).astype(BF16)

        qm = z[:, Z_QM:Z_KPE]
        if do_mem:
            mbc_ref[rows, 0:CONV_W] = mb
            return _mem_scores(qm, mk_ref[0, 0], rs)
        mb_ref[rows, :] = mb
        qm_ref[rows, :] = qm
        return None

    def mem_finish(t, st):
        if do_mem:
            rows = slice(t * rs, (t + 1) * rs)
            om = _mem_values(st[0], st[1], mv_ref[0, 0], rs)
            mbc_ref[rows, CONV_W:] = _rms(om, ng[:, MLA_W + CONV_W:]).astype(BF16)

    z = project(0)
    for t in range(IN_SUBTILES):
        st = mix(t, z)
        if t + 1 < IN_SUBTILES:
            z = project(t + 1)
        mem_finish(t, st)
    newc = ubuf[base + tm - 2 * s:base + tm, :]
    ubuf[base - 2 * s:base, :] = newc
    cnew_ref[0] = newc


def _in_proj(l, x, nb, t_len, tm, s, tabs, wts, carry0=None, mem_kv=None):
    nt = t_len // tm
    rows = nb * t_len
    has_carry = carry0 is not None
    do_mem = mem_kv is not None
    row_map = lambda b, i: (b * nt + i, 0)
    tab_map = lambda b, i: (i, 0)
    lay = lambda b, i: (l, 0, 0)
    in_specs = [
        pl.BlockSpec((tm, D_MODEL), row_map),
        pl.BlockSpec((tm, 256), tab_map),
        pl.BlockSpec((tm, 256), tab_map),
        pl.BlockSpec((tm, 128), tab_map),
        pl.BlockSpec((1, 1, D_MODEL), lay),
        _resident((1, D_MODEL, Z_COLS), lay),
        pl.BlockSpec((1, 1, Q_LORA), lay),
        _resident((1, Q_LORA, 1024), lay),
        pl.BlockSpec((1, 1, KV_LORA), lay),
        _resident((1, MLA_HEADS // 2, 128, 2 * KV_LORA), lambda b, i: (l, 0, 0, 0)),
        pl.BlockSpec((1, 3, CONV_W), lay),
        pl.BlockSpec((1, 1, D_MODEL), lay),
    ]
    args = [x, *tabs, wts["norm_pre_mix"], wts["w_in"], wts["norm_q"], wts["w_uq"],
            wts["norm_kv"], wts["w_uk2"], wts["conv_w"], wts["norm_group"]]
    if has_carry:
        in_specs.append(pl.BlockSpec((1, 2 * s, CONV_W), lambda b, i: (b, 0, 0)))
        args.append(carry0)
    if do_mem:
        mspec = pl.BlockSpec((1, 1, MEM_W, MEM_TOKENS), lambda b, i: (l, b, 0, 0))
        in_specs += [mspec, mspec]
        args += list(mem_kv)
    out_shape = [
        jax.ShapeDtypeStruct((nb, MLA_HEADS, t_len, KV_LORA), BF16),
        jax.ShapeDtypeStruct((nb, t_len, 256), BF16),
        jax.ShapeDtypeStruct((nb, t_len, 512), BF16),
        jax.ShapeDtypeStruct((nb, KV_LORA, t_len), BF16),
        jax.ShapeDtypeStruct((rows, KV_LORA), F32),
        jax.ShapeDtypeStruct((nb, MLA_ROPE, t_len), F32),
    ]
    out_specs = [
        pl.BlockSpec((1, MLA_HEADS, tm, KV_LORA), lambda b, i: (b, 0, i, 0)),
        pl.BlockSpec((1, tm, 256), lambda b, i: (b, i, 0)),
        pl.BlockSpec((1, tm, 512), lambda b, i: (b, i, 0)),
        pl.BlockSpec((1, KV_LORA, tm), lambda b, i: (b, 0, i)),
        pl.BlockSpec((tm, KV_LORA), row_map),
        pl.BlockSpec((1, MLA_ROPE, tm), lambda b, i: (b, 0, i)),
    ]
    if do_mem:
        out_shape.append(jax.ShapeDtypeStruct((rows, 512), BF16))
        out_specs.append(pl.BlockSpec((tm, 512), row_map))
    else:
        out_shape += [jax.ShapeDtypeStruct((rows, CONV_W), BF16),
                      jax.ShapeDtypeStruct((rows, MEM_W), F32)]
        out_specs += [pl.BlockSpec((tm, CONV_W), row_map), pl.BlockSpec((tm, MEM_W), row_map)]
    out_shape.append(jax.ShapeDtypeStruct((nb, 2 * s, CONV_W), F32))
    out_specs.append(pl.BlockSpec((1, 2 * s, CONV_W), lambda b, i: (b, 0, 0)))
    kern = functools.partial(_in_proj_kernel, tm=tm, s=s, has_carry=has_carry, do_mem=do_mem)
    return pl.pallas_call(
        kern,
        grid=(nb, nt),
        in_specs=in_specs,
        out_specs=out_specs,
        out_shape=out_shape,
        scratch_shapes=[pltpu.VMEM((_carry_base(s) + tm, CONV_W), F32)],
        compiler_params=_params("parallel", "arbitrary"),
        name="in_proj_mem" if do_mem else "in_proj",
    )(*args)


def _causal_pairs(nq, tq, tk):
    it, kt = [], []
    for i in range(nq):
        for k in range((i * tq + tq - 1) // tk + 1):
            it.append(i)
            kt.append(k)
    return np.asarray(it, np.int32), np.asarray(kt, np.int32)


def _attn_kernel(it_ref, kt_ref, qlat_ref, qpe_ref, kall_ref, ckvt_ref, wuvt_ref, ngb_ref, o_ref,
                 q2, m_scr, l_scr, acct, *, tq, tk):
    assert tq == tk == 2 * ATTN_HALF
    j = pl.program_id(1)
    i = it_ref[j]
    k = kt_ref[j]
    hc = MLA_HEADS * ATTN_HALF
    cols = 2 * hc
    c_exp = MLA_SCALE * LOG2E

    @pl.when(k == 0)
    def _():
        lane = lax.broadcasted_iota(jnp.int32, (ATTN_HALF, 256), 1)
        for th in range(2):
            tr = slice(th * ATTN_HALF, (th + 1) * ATTN_HALF)
            qp = qpe_ref[0, tr, :]
            zero = jnp.zeros_like(qp)
            for h in range(MLA_HEADS):
                r0 = th * hc + h * ATTN_HALF
                q2[r0:r0 + ATTN_HALF, 0:KV_LORA] = qlat_ref[0, h, tr, :]
                q2[r0:r0 + ATTN_HALF, KV_LORA:] = jnp.where(
                    _head_mask(lane, h, MLA_ROPE), qp, zero)
        m_scr[...] = jnp.full((1, cols), NEG, F32)
        l_scr[...] = jnp.zeros((1, cols), F32)
        acct[...] = jnp.zeros((KV_LORA, cols), F32)

    def step(sc, cs, vt):
        m_prev = m_scr[:, cs]
        m_new = jnp.maximum(m_prev, jnp.max(sc, axis=0, keepdims=True))
        alpha = jnp.exp2((m_prev - m_new) * c_exp)
        p = jnp.exp2((sc - m_new) * c_exp)
        l_scr[:, cs] = alpha * l_scr[:, cs] + jnp.sum(p, axis=0, keepdims=True)
        acct[:, cs] = alpha * acct[:, cs] + _dot(vt, p.astype(BF16))
        m_scr[:, cs] = m_new

    @pl.when(k < i)
    def _():
        step(_dot_nt(kall_ref[0], q2[...]), slice(0, cols), ckvt_ref[0])

    @pl.when(k == i)
    def _():
        first = slice(0, ATTN_HALF)
        second = slice(ATTN_HALF, 2 * ATTN_HALF)
        row = lax.broadcasted_iota(jnp.int32, (ATTN_HALF, cols), 0)
        col = lax.broadcasted_iota(jnp.int32, (ATTN_HALF, cols), 1)
        ok = (col >= hc) | (row <= (col & (ATTN_HALF - 1)))
        sc = _dot_nt(kall_ref[0, first, :], q2[...])
        step(jnp.where(ok, sc, NEG), slice(0, cols), ckvt_ref[0, :, first])
        row = lax.broadcasted_iota(jnp.int32, (ATTN_HALF, hc), 0)
        col = lax.broadcasted_iota(jnp.int32, (ATTN_HALF, hc), 1)
        sc = _dot_nt(kall_ref[0, second, :], q2[hc:cols, :])
        step(jnp.where(row <= (col & (ATTN_HALF - 1)), sc, NEG), slice(hc, cols),
             ckvt_ref[0, :, second])

        ot = (acct[...] * (1.0 / l_scr[...])).astype(BF16)
        omt = jnp.concatenate([
            jnp.concatenate([
                _dot(wuvt_ref[0, h], ot[:, th * hc + h * ATTN_HALF:th * hc + (h + 1) * ATTN_HALF])
                for h in range(MLA_HEADS)], axis=0)
            for th in range(2)], axis=1)
        y = omt * lax.rsqrt(jnp.mean(omt * omt, axis=0, keepdims=True) + EPS) * ngb_ref[0]
        o_ref[...] = y.T.astype(BF16)


def _prompt_attention(l, qlat, qpe, kall, ckvt, wts):
    nb, _, t_len, _ = qlat.shape
    tq, tk = ATTN_TQ, ATTN_TK
    nq = t_len // tq
    cols = MLA_HEADS * tq
    it, kt = _causal_pairs(nq, tq, tk)
    kern = functools.partial(_attn_kernel, tq=tq, tk=tk)
    grid_spec = pltpu.PrefetchScalarGridSpec(
        num_scalar_prefetch=2,
        grid=(nb, it.shape[0]),
        in_specs=[
            pl.BlockSpec((1, MLA_HEADS, tq, KV_LORA), lambda b, j, it, kt: (b, 0, it[j], 0)),
            pl.BlockSpec((1, tq, 256), lambda b, j, it, kt: (b, it[j], 0)),
            pl.BlockSpec((1, tk, 512), lambda b, j, it, kt: (b, kt[j], 0)),
            pl.BlockSpec((1, KV_LORA, tk), lambda b, j, it, kt: (b, 0, kt[j])),
            _resident((1, MLA_HEADS, MLA_V, KV_LORA), lambda b, j, it, kt: (l, 0, 0, 0)),
            pl.BlockSpec((1, MLA_W, tq), lambda b, j, it, kt: (l, 0, 0)),
        ],
        out_specs=pl.BlockSpec((tq, MLA_W), lambda b, j, it, kt: (b * nq + it[j], 0)),
        scratch_shapes=[
            pltpu.VMEM((cols, 512), BF16),
            pltpu.VMEM((1, cols), F32),
            pltpu.VMEM((1, cols), F32),
            pltpu.VMEM((KV_LORA, cols), F32),
        ],
    )
    return pl.pallas_call(
        kern,
        grid_spec=grid_spec,
        out_shape=jax.ShapeDtypeStruct((nb * t_len, MLA_W), BF16),
        compiler_params=_params("parallel", "arbitrary"),
        name="prompt_attn",
    )(jnp.asarray(it), jnp.asarray(kt), qlat, qpe, kall, ckvt, wts["w_uv_t"], wts["ng_attn"])


def _sample_attn_kernel(pt_ref, qlat_ref, qpe_ref, cnew_ref, pnew_ref, ckv_hbm, kpe_hbm, o_ref,
                        kbuf, pbuf, kb16, s_scr, sem, *, l, n_pages, page):
    b = pl.program_id(0)
    nb = pl.num_programs(0)
    slot = b % 2
    past = n_pages * page

    def page_copies(seq, sl, p):
        pg = pt_ref[seq, p]
        dst = pl.ds(pl.multiple_of(p * page, page), page)
        return (pltpu.make_async_copy(ckv_hbm.at[l, pg], kbuf.at[sl, dst, :], sem.at[0, sl]),
                pltpu.make_async_copy(kpe_hbm.at[l, pg], pbuf.at[sl, :, dst], sem.at[1, sl]))

    def issue(seq, sl):
        def body(p, c):
            for cp in page_copies(seq, sl, p):
                cp.start()
            return c
        lax.fori_loop(0, n_pages, body, 0, unroll=min(DMA_UNROLL, n_pages))

    @pl.when(b == 0)
    def _():
        issue(0, 0)

    @pl.when(b + 1 < nb)
    def _():
        issue(b + 1, 1 - slot)

    first = pl.ds(0, page)
    for _ in range(n_pages):
        pltpu.make_async_copy(ckv_hbm.at[l, 0], kbuf.at[slot, first, :], sem.at[0, slot]).wait()
    for _ in range(n_pages):
        pltpu.make_async_copy(kpe_hbm.at[l, 0], pbuf.at[slot, :, first], sem.at[1, slot]).wait()

    q = qlat_ref[0]
    qp = qpe_ref[0]
    nrow = q.shape[0]
    nch = past // PAGE_CHUNK
    for c in range(nch):
        sl_c = slice(c * PAGE_CHUNK, (c + 1) * PAGE_CHUNK)
        kb16[sl_c, :] = kbuf[slot, sl_c, :].astype(BF16)
    s_scr[...] = (_dot_nt(q, kb16[...]) + _dot(qp, pbuf[slot].astype(BF16))) * MLA_SCALE

    @pl.when(b >= 0)
    def _():
        cn = cnew_ref[0].astype(BF16)
        pn = pnew_ref[0].astype(BF16)
        s_new = (_dot_nt(q, cn) + _dot(qp, pn)) * MLA_SCALE
        t_row = lax.broadcasted_iota(jnp.int32, (nrow, NEW_PAD), 0) // MLA_HEADS
        j_col = lax.broadcasted_iota(jnp.int32, (nrow, NEW_PAD), 1)
        s_new = jnp.where(j_col <= t_row, s_new, NEG)
        s_all = s_scr[...]
        m = jnp.maximum(jnp.max(s_all, axis=-1, keepdims=True),
                        jnp.max(s_new, axis=-1, keepdims=True))
        p_new = jnp.exp(s_new - m)
        p_all = jnp.exp(s_all - m)
        denom = jnp.sum(p_all, axis=-1, keepdims=True) + jnp.sum(p_new, axis=-1, keepdims=True)
        s_scr[...] = p_all
        accs = [_dot(p_new.astype(BF16), cn), None]
        for c in range(nch):
            sl_c = slice(c * PAGE_CHUNK, (c + 1) * PAGE_CHUNK)
            d = _dot(s_scr[:, sl_c].astype(BF16), kb16[sl_c, :])
            accs[c % 2] = d if accs[c % 2] is None else accs[c % 2] + d
        acc = accs[0] if accs[1] is None else accs[0] + accs[1]
        o_ref[0] = acc / denom


def _sample_attention(l, page_table, qlat, qpe, cnew, pnew_t, cache_ckv, cache_kpe_t):
    nseq, nrow, _ = qlat.shape
    n_pages = page_table.shape[1]
    page = cache_ckv.shape[2]
    past = n_pages * page
    kern = functools.partial(_sample_attn_kernel, l=l, n_pages=n_pages, page=page)
    grid_spec = pltpu.PrefetchScalarGridSpec(
        num_scalar_prefetch=1,
        grid=(nseq,),
        in_specs=[
            pl.BlockSpec((1, nrow, KV_LORA), lambda b, pt: (b, 0, 0)),
            pl.BlockSpec((1, nrow, MLA_ROPE), lambda b, pt: (b, 0, 0)),
            pl.BlockSpec((1, NEW_PAD, KV_LORA), lambda b, pt: (b, 0, 0)),
            pl.BlockSpec((1, MLA_ROPE, NEW_PAD), lambda b, pt: (b, 0, 0)),
            pl.BlockSpec(memory_space=pl.ANY),
            pl.BlockSpec(memory_space=pl.ANY),
        ],
        out_specs=pl.BlockSpec((1, nrow, KV_LORA), lambda b, pt: (b, 0, 0)),
        scratch_shapes=[
            pltpu.VMEM((2, past, KV_LORA), F32),
            pltpu.VMEM((2, MLA_ROPE, past), F32),
            pltpu.VMEM((past, KV_LORA), BF16),
            pltpu.VMEM((nrow, past), F32),
            pltpu.SemaphoreType.DMA((2, 2)),
        ],
    )
    return pl.pallas_call(
        kern,
        grid_spec=grid_spec,
        out_shape=jax.ShapeDtypeStruct((nseq, nrow, KV_LORA), F32),
        compiler_params=_params("arbitrary"),
        name="sample_attn",
    )(page_table, qlat, qpe, cnew, pnew_t, cache_ckv, cache_kpe_t)


def _uv_kernel(o_ref, w_ref, ng_ref, out_ref):
    om = _dot(o_ref[...].astype(BF16), w_ref[0])
    out_ref[...] = _rms(om, ng_ref[0][:, 0:MLA_W]).astype(BF16)


def _uv_proj(l, o_lat, wts):
    rows = o_lat.shape[0]
    return pl.pallas_call(
        _uv_kernel,
        grid=(1,),
        in_specs=[
            pl.BlockSpec((rows, MLA_HEADS * KV_LORA), lambda i: (0, 0)),
            pl.BlockSpec((1, MLA_HEADS * KV_LORA, MLA_W), lambda i: (l, 0, 0)),
            pl.BlockSpec((1, 1, D_MODEL), lambda i: (l, 0, 0)),
        ],
        out_specs=pl.BlockSpec((rows, MLA_W), lambda i: (0, 0)),
        out_shape=jax.ShapeDtypeStruct((rows, MLA_W), BF16),
        compiler_params=_params("arbitrary"),
        name="uv_proj",
    )(o_lat, wts["w_uv_flat"], wts["norm_group"])


def _sample_mem_kernel(q_ref, mk_ref, mv_ref, ng_ref, o_ref):
    ng = ng_ref[0]
    for j in range(SMEM_BLOCK):
        om = _mem_attend(q_ref[j], mk_ref[0, j], mv_ref[0, j], SUBLANES)
        o_ref[j] = _rms(om, ng[:, MLA_W + CONV_W:])


def _sample_mem(l, q, mem_k, mem_v, wts):
    nseq = q.shape[0]
    mspec = pl.BlockSpec((1, SMEM_BLOCK, MEM_W, MEM_TOKENS), lambda j: (l, j, 0, 0))
    return pl.pallas_call(
        _sample_mem_kernel,
        grid=(nseq // SMEM_BLOCK,),
        in_specs=[
            pl.BlockSpec((SMEM_BLOCK, SUBLANES, MEM_W), lambda j: (j, 0, 0)),
            mspec, mspec,
            pl.BlockSpec((1, 1, D_MODEL), lambda j: (l, 0, 0)),
        ],
        out_specs=pl.BlockSpec((SMEM_BLOCK, SUBLANES, MEM_W), lambda j: (j, 0, 0)),
        out_shape=jax.ShapeDtypeStruct((nseq, SUBLANES, MEM_W), F32),
        compiler_params=_params("parallel"),
        name="sample_mem",
    )(q, mem_k, mem_v, wts["norm_group"])


def _ffn_kernel(*refs, tm, s, has_carry, widths):
    it = iter(refs)
    x_ref = next(it)
    piece_refs = [next(it) for _ in widths]
    (wout_ref, npost_ref, npre_ref, wg_ref, wu_ref, fcw_ref, fcb_ref, wd_ref,
     npf_ref) = [next(it) for _ in range(9)]
    c0_ref = next(it) if has_carry else None
    y_ref, fnew_ref, gbuf = next(it), next(it), next(it)

    i = pl.program_id(1)
    base = _carry_base(s)
    rs = tm // FFN_SUBTILES
    nchunk = D_FF // FF_CHUNK
    cols = lambda c: slice(c * FF_CHUNK, (c + 1) * FF_CHUNK)

    @pl.when(i == 0)
    def _():
        if has_carry:
            gbuf[base - 2 * s:base, :] = c0_ref[0]
        else:
            gbuf[0:base, :] = jnp.zeros((base, D_FF), F32)

    w = fcw_ref[0]
    bias = fcb_ref[0]

    def prologue(t):
        rows = slice(t * rs, (t + 1) * rs)
        att = None
        off = 0
        for pref, wd in zip(piece_refs, widths):
            part = _dot(pref[rows, :], wout_ref[0, off:off + wd, :])
            att = part if att is None else att + part
            off += wd
        x1 = x_ref[rows, :] + _rms(att, npost_ref[0])
        return x1, _rms(x1, npre_ref[0]).astype(BF16)

    def gate_up(h2, c):
        return _dot(h2, wg_ref[0, :, cols(c)]), _dot(h2, wu_ref[0, :, cols(c)])

    def act(t, c, g, up):
        r0 = base + t * rs
        cs = cols(c)
        gbuf[r0:r0 + rs, cs] = g
        gc = (w[0:1, cs] * gbuf[r0 - 2 * s:r0 - 2 * s + rs, cs]
              + w[1:2, cs] * gbuf[r0 - s:r0 - s + rs, cs] + w[2:3, cs] * g + bias[:, cs])
        return (gc * (1.0 / (1.0 + jnp.exp(-gc))) * up).astype(BF16)

    x1s, h2s = [None] * FFN_SUBTILES, [None] * FFN_SUBTILES
    x1s[0], h2s[0] = prologue(0)
    items = [(t, c) for t in range(FFN_SUBTILES) for c in range(nchunk)]
    nxt = gate_up(h2s[0], 0)
    dn = None
    for idx, (t, c) in enumerate(items):
        g, up = nxt
        if idx == 0:
            for t2 in range(1, FFN_SUBTILES):
                x1s[t2], h2s[t2] = prologue(t2)
        if idx + 1 < len(items):
            t_n, c_n = items[idx + 1]
            nxt = gate_up(h2s[t_n], c_n)
        part = _dot(act(t, c, g, up), wd_ref[0, cols(c), :])
        dn = part if c == 0 else dn + part
        if c == nchunk - 1:
            y_ref[t * rs:(t + 1) * rs, :] = x1s[t] + _rms(dn, npf_ref[0])
    newc = gbuf[base + tm - 2 * s:base + tm, :]
    gbuf[base - 2 * s:base, :] = newc
    fnew_ref[0] = newc


def _out_ffn(l, x, pieces, nb, t_len, tm, s, wts, carry0=None):
    nt = t_len // tm
    rows = nb * t_len
    has_carry = carry0 is not None
    widths = tuple(p.shape[1] for p in pieces)
    row_map = lambda b, i: (b * nt + i, 0)
    lay = lambda b, i: (l, 0, 0)
    in_specs = [pl.BlockSpec((tm, D_MODEL), row_map)]
    in_specs += [pl.BlockSpec((tm, wd), row_map) for wd in widths]
    in_specs += [
        _resident((1, D_MODEL, D_MODEL), lay),
        pl.BlockSpec((1, 1, D_MODEL), lay),
        pl.BlockSpec((1, 1, D_MODEL), lay),
        _resident((1, D_MODEL, D_FF), lay),
        _resident((1, D_MODEL, D_FF), lay),
        pl.BlockSpec((1, 3, D_FF), lay),
        pl.BlockSpec((1, 1, D_FF), lay),
        _resident((1, D_FF, D_MODEL), lay),
        pl.BlockSpec((1, 1, D_MODEL), lay),
    ]
    args = [x, *pieces, wts["w_out"], wts["norm_post_mix"], wts["norm_pre_ffn"], wts["w_gate"],
            wts["w_up"], wts["ffn_conv_w"], wts["ffn_conv_b"], wts["w_down"],
            wts["norm_post_ffn"]]
    if has_carry:
        in_specs.append(pl.BlockSpec((1, 2 * s, D_FF), lambda b, i: (b, 0, 0)))
        args.append(carry0)
    kern = functools.partial(_ffn_kernel, tm=tm, s=s, has_carry=has_carry, widths=widths)
    return pl.pallas_call(
        kern,
        grid=(nb, nt),
        in_specs=in_specs,
        out_specs=[pl.BlockSpec((tm, D_MODEL), row_map),
                   pl.BlockSpec((1, 2 * s, D_FF), lambda b, i: (b, 0, 0))],
        out_shape=[jax.ShapeDtypeStruct((rows, D_MODEL), F32),
                   jax.ShapeDtypeStruct((nb, 2 * s, D_FF), F32)],
        scratch_shapes=[pltpu.VMEM((_carry_base(s) + tm, D_FF), F32)],
        compiler_params=_params("parallel", "arbitrary"),
        name="out_ffn",
    )(*args)


def _rot_pair(w):
    half = MLA_ROPE // 2
    return jnp.concatenate([-w[..., half:], w[..., :half]], axis=-1)


def _prep_weights(norm_pre_mix, w_in, norm_q, w_uq, norm_kv, w_uk, w_uv, conv_w, norm_mem,
                  w_mem_k, w_mem_v, norm_group, w_out, norm_post_mix, norm_pre_ffn, w_gate, w_up,
                  ffn_conv_w, ffn_conv_b, w_down, norm_post_ffn):
    o0 = Q_LORA
    o1 = o0 + KV_LORA
    o2 = o1 + MLA_ROPE
    o3 = o2 + CONV_W
    o4 = o3 + CONV_W
    o5 = o4 + CONV_W
    kpe_w = w_in[..., o1:o2]
    w_in_r = jnp.concatenate(
        [w_in[..., :o1], w_in[..., o2:], kpe_w, _rot_pair(kpe_w),
         jnp.zeros(w_in.shape[:2] + (128 - 2 * MLA_ROPE,), w_in.dtype)], axis=-1).astype(BF16)

    per_head = w_uq.reshape(DEPTH, Q_LORA, MLA_HEADS, MLA_NOPE + MLA_ROPE)
    nope = per_head[..., :MLA_NOPE].reshape(DEPTH, Q_LORA, MLA_HEADS * MLA_NOPE)
    pe = per_head[..., MLA_NOPE:]
    w_uq_r = jnp.concatenate(
        [nope, pe.reshape(DEPTH, Q_LORA, -1), _rot_pair(pe).reshape(DEPTH, Q_LORA, -1)],
        axis=-1).astype(BF16)

    wt = jnp.transpose(w_uk, (0, 2, 3, 1)).reshape(DEPTH, MLA_HEADS // 2, 2, MLA_NOPE, KV_LORA)
    w_uk2 = jnp.einsum("ljanr,ab->ljanbr", wt, jnp.eye(2, dtype=wt.dtype)).reshape(
        DEPTH, MLA_HEADS // 2, 2 * MLA_NOPE, 2 * KV_LORA).astype(BF16)

    w_uv_exp = jnp.einsum("lrhv,hg->lhrgv", w_uv, jnp.eye(MLA_HEADS, dtype=w_uv.dtype)).reshape(
        DEPTH, MLA_HEADS, KV_LORA, MLA_W).astype(BF16)

    vec = lambda a: a.reshape(DEPTH, 1, a.shape[-1])
    return dict(
        norm_pre_mix=vec(norm_pre_mix), w_in=w_in_r, norm_q=vec(norm_q), w_uq=w_uq_r,
        norm_kv=vec(norm_kv), w_uk2=w_uk2,
        w_uv_t=jnp.transpose(w_uv, (0, 2, 3, 1)).astype(BF16),
        ng_attn=jnp.broadcast_to(norm_group[:, :MLA_W, None], (DEPTH, MLA_W, ATTN_TQ)),
        w_uv_flat=w_uv_exp.reshape(DEPTH, MLA_HEADS * KV_LORA, MLA_W), conv_w=conv_w,
        norm_mem=vec(norm_mem), w_mem_k=jnp.transpose(w_mem_k, (0, 2, 1)).astype(BF16),
        w_mem_v=jnp.transpose(w_mem_v, (0, 2, 1)).astype(BF16),
        norm_group=vec(norm_group), w_out=w_out.astype(BF16), norm_post_mix=vec(norm_post_mix),
        norm_pre_ffn=vec(norm_pre_ffn), w_gate=w_gate.astype(BF16), w_up=w_up.astype(BF16),
        ffn_conv_w=ffn_conv_w, ffn_conv_b=vec(ffn_conv_b), w_down=w_down.astype(BF16),
        norm_post_ffn=vec(norm_post_ffn))


def _rope_tables(pos):
    half = MLA_ROPE // 2
    freqs = ROPE_THETA ** (-jnp.arange(half, dtype=F32) * (2.0 / MLA_ROPE))
    ang = pos.astype(F32)[:, None] * freqs[None, :]
    cos = jnp.cos(ang)
    sin = jnp.sin(ang)
    cos32 = jnp.concatenate([cos, cos], axis=-1)
    sin32 = jnp.concatenate([sin, sin], axis=-1)
    tabk = jnp.concatenate([cos32, sin32, jnp.zeros((pos.shape[0], 64), F32)], axis=-1)
    return jnp.tile(cos32, (1, MLA_HEADS)), jnp.tile(sin32, (1, MLA_HEADS)), tabk


def _to_time_major(a):
    return jnp.transpose(a, (1, 0, 2)).reshape(-1, a.shape[-1])


def _from_time_major(a, t_len):
    return jnp.transpose(a.reshape(t_len, -1, a.shape[-1]), (1, 0, 2))


def kernel(x_prompt, x_sample, mem_prompt, cache_ckv, cache_kpe, cache_mem_k, cache_mem_v, state_conv, state_ffn_conv, page_table, norm_pre_mix, w_in, norm_q, w_uq, norm_kv, w_uk, w_uv, conv_w, norm_mem, w_mem_k, w_mem_v, norm_group, w_out, norm_post_mix, norm_pre_ffn, w_gate, w_up, ffn_conv_w, ffn_conv_b, w_down, norm_post_ffn):
    nb, seq, _ = x_prompt.shape
    nseq, dseq, _ = x_sample.shape
    n_pages = page_table.shape[1]
    page = cache_ckv.shape[2]
    past = n_pages * page
    wts = _prep_weights(norm_pre_mix, w_in, norm_q, w_uq, norm_kv, w_uk, w_uv, conv_w, norm_mem,
                        w_mem_k, w_mem_v, norm_group, w_out, norm_post_mix, norm_pre_ffn, w_gate,
                        w_up, ffn_conv_w, ffn_conv_b, w_down, norm_post_ffn)

    tabs_p = _rope_tables(jnp.arange(seq, dtype=jnp.int32))
    pos_s = jnp.repeat(past + jnp.arange(dseq, dtype=jnp.int32), nseq)
    tabs_s = _rope_tables(pos_s)

    mem_k_all, mem_v_all = _mem_kv(mem_prompt, wts["norm_mem"], wts["w_mem_k"], wts["w_mem_v"])
    smem_k = jnp.transpose(cache_mem_k, (0, 1, 3, 4, 2)).reshape(DEPTH, nseq, MEM_W, MEM_TOKENS)
    smem_v = jnp.transpose(cache_mem_v, (0, 1, 3, 4, 2)).reshape(DEPTH, nseq, MEM_W, MEM_TOKENS)
    cache_kpe_t = jnp.transpose(cache_kpe, (0, 1, 3, 2))

    yp = x_prompt.reshape(nb * seq, D_MODEL)
    ys = _to_time_major(x_sample)
    srows = nseq * dseq
    p_ckv, p_kpe, p_conv, p_ffn = [], [], [], []
    s_ckv, s_kpe, s_conv, s_ffn = [], [], [], []
    for l in range(DEPTH):
        qlat, qpe, kall, ckvt, ckv, kpt, mbc, cnew = _in_proj(
            l, yp, nb, seq, IN_TM, 1, tabs_p, wts,
            mem_kv=(mem_k_all, mem_v_all))
        ma = _prompt_attention(l, qlat, qpe, kall, ckvt, wts)
        yp, fnew = _out_ffn(l, yp, [ma, mbc], nb, seq, FFN_TM, 1, wts)
        p_ckv.append(ckv.reshape(nb, seq, KV_LORA))
        p_kpe.append(kpt)
        p_conv.append(cnew)
        p_ffn.append(fnew)

        c0 = _to_time_major(state_conv[l])[None]
        f0 = _to_time_major(state_ffn_conv[l])[None]
        qlat, qpe, _, _, ckv, kpt, mb, qm, cnew = _in_proj(
            l, ys, 1, srows, srows, nseq, tabs_s, wts, carry0=c0)
        ckv_bt = _from_time_major(ckv, dseq)
        kp_dtb = kpt[0].reshape(MLA_ROPE, dseq, nseq)
        qlat_bt = jnp.transpose(qlat[0].reshape(MLA_HEADS, dseq, nseq, KV_LORA),
                                (2, 1, 0, 3)).reshape(nseq, dseq * MLA_HEADS, KV_LORA)
        qpe_bt = jnp.transpose(qpe[0].reshape(dseq, nseq, MLA_HEADS, MLA_ROPE),
                               (1, 0, 2, 3)).reshape(nseq, dseq * MLA_HEADS, MLA_ROPE)
        cnew16 = jnp.pad(ckv_bt, ((0, 0), (0, NEW_PAD - dseq), (0, 0)))
        pnew16 = jnp.pad(jnp.transpose(kp_dtb, (2, 0, 1)), ((0, 0), (0, 0), (0, NEW_PAD - dseq)))
        o_lat = _sample_attention(l, page_table, qlat_bt, qpe_bt, cnew16, pnew16,
                                  cache_ckv, cache_kpe_t)
        o_lat = _to_time_major(o_lat.reshape(nseq, dseq, MLA_HEADS * KV_LORA))
        ma = _uv_proj(l, o_lat, wts)
        qm_bt = jnp.pad(_from_time_major(qm, dseq), ((0, 0), (0, SUBLANES - dseq), (0, 0)))
        mc = _sample_mem(l, qm_bt, smem_k, smem_v, wts)
        mc = _to_time_major(mc[:, :dseq]).astype(BF16)
        ys, fnew = _out_ffn(l, ys, [ma, mb, mc], 1, srows, srows, nseq, wts, carry0=f0)
        s_ckv.append(ckv_bt)
        s_kpe.append(jnp.transpose(kp_dtb, (2, 1, 0)))
        s_conv.append(_from_time_major(cnew[0], 2))
        s_ffn.append(_from_time_major(fnew[0], 2))

    def mem_out(a):
        return jnp.transpose(a.reshape(DEPTH, nb, MEM_HEADS, HEAD_DIM, MEM_TOKENS), (0, 1, 4, 2, 3))

    return (yp.reshape(nb, seq, D_MODEL), _from_time_major(ys, dseq),
            jnp.stack(p_ckv), jnp.transpose(jnp.stack(p_kpe), (0, 1, 3, 2)),
            mem_out(mem_k_all), mem_out(mem_v_all),
            jnp.stack(p_conv), jnp.stack(p_ffn),
            jnp.stack(s_ckv), jnp.stack(s_kpe), jnp.stack(s_conv), jnp.stack(s_ffn))
```

```python
import functools

import jax
import jax.numpy as jnp
import numpy as np
from jax import lax
from jax.experimental import pallas as pl
from jax.experimental.pallas import tpu as pltpu

F32 = jnp.float32
BF16 = jnp.bfloat16

D_MODEL = 1024
DEPTH = 4
HEAD_DIM = 64
MLA_HEADS = 8
MLA_NOPE = 64
MLA_ROPE = 32
MLA_V = 64
MLA_W = MLA_HEADS * MLA_V
KV_LORA = 256
Q_LORA = 384
CONV_W = 256
MEM_HEADS = 4
MEM_W = MEM_HEADS * HEAD_DIM
MEM_TOKENS = 256
D_FF = 2816
ROPE_THETA = 10000.0
EPS = 1e-6
MLA_SCALE = (MLA_NOPE + MLA_ROPE) ** -0.5
MEM_SCALE = HEAD_DIM ** -0.5
NEG = -1e30
LOG2E = 1.4426950408889634

Z_CQ = 0
Z_CKV = Z_CQ + Q_LORA
Z_GB = Z_CKV + KV_LORA
Z_GC = Z_GB + CONV_W
Z_HV = Z_GC + CONV_W
Z_QM = Z_HV + CONV_W
Z_KPE = Z_QM + MEM_W
Z_COLS = Z_KPE + 128

SUBLANES = 8
VMEM_LIMIT_BYTES = 56 * 1024 * 1024

PROMPT_TM = 256
ATTN_TQ = 512
ATTN_TK = 512
ATTN_HALF = 256
PAGE_CHUNK = 2048
NEW_PAD = 16
IN_TM = 512
IN_SUBTILES = 2
FF_CHUNK = 256
FFN_SUBTILES = 2
FFN_TM = 512
SMEM_BLOCK = 8
DMA_UNROLL = 64


def _rms(x, g):
    return x * lax.rsqrt(jnp.mean(x * x, axis=-1, keepdims=True) + EPS) * g


def _dot(a, b):
    return jnp.dot(a, b, preferred_element_type=F32)


def _dot_nt(a, b):
    return lax.dot_general(a, b, (((1,), (1,)), ((), ())), preferred_element_type=F32)


def _resident(shape, index_map):
    return pl.BlockSpec(shape, index_map, pipeline_mode=pl.Buffered(1))


def _params(*sem):
    return pltpu.CompilerParams(dimension_semantics=sem, vmem_limit_bytes=VMEM_LIMIT_BYTES)


def _head_mask(lane, h, width):
    return (lane >= width * h) & (lane < width * (h + 1))


def _carry_base(s):
    return -(-2 * s // SUBLANES) * SUBLANES


def _mem_kv_kernel(mem_ref, g_ref, wk_ref, wv_ref, k_ref, v_ref):
    m = _rms(mem_ref[0], g_ref[0]).astype(BF16)
    k_ref[0, 0] = _dot_nt(wk_ref[0], m)
    v_ref[0, 0] = _dot_nt(wv_ref[0], m)


def _mem_kv(mem, norm_mem, wk, wv):
    nb = mem.shape[0]
    out = jax.ShapeDtypeStruct((DEPTH, nb, MEM_W, MEM_TOKENS), F32)
    return pl.pallas_call(
        _mem_kv_kernel,
        grid=(DEPTH, nb),
        in_specs=[
            pl.BlockSpec((1, MEM_TOKENS, D_MODEL), lambda l, b: (b, 0, 0)),
            pl.BlockSpec((1, 1, D_MODEL), lambda l, b: (l, 0, 0)),
            pl.BlockSpec((1, MEM_W, D_MODEL), lambda l, b: (l, 0, 0)),
            pl.BlockSpec((1, MEM_W, D_MODEL), lambda l, b: (l, 0, 0)),
        ],
        out_specs=[
            pl.BlockSpec((1, 1, MEM_W, MEM_TOKENS), lambda l, b: (l, b, 0, 0)),
            pl.BlockSpec((1, 1, MEM_W, MEM_TOKENS), lambda l, b: (l, b, 0, 0)),
        ],
        out_shape=[out, out],
        compiler_params=_params("parallel", "parallel"),
        name="mem_kv",
    )(mem, norm_mem, wk, wv)


def _mem_scores(qm, mk_t, rows):
    lane = lax.broadcasted_iota(jnp.int32, (rows, MEM_W), 1)
    masks = [_head_mask(lane, h, HEAD_DIM) for h in range(MEM_HEADS)]
    qs = jnp.concatenate([jnp.where(m, qm, 0.0) for m in masks], axis=0).astype(BF16)
    return _dot(qs, mk_t.astype(BF16)) * MEM_SCALE, masks


def _mem_values(sc, masks, mv_t, rows):
    p = jnp.exp(sc - jnp.max(sc, axis=-1, keepdims=True))
    o = _dot_nt(p.astype(BF16), mv_t.astype(BF16)) / jnp.sum(p, axis=-1, keepdims=True)
    om = jnp.where(masks[0], o[0:rows], 0.0)
    for h in range(1, MEM_HEADS):
        om = om + jnp.where(masks[h], o[h * rows:(h + 1) * rows], 0.0)
    return om


def _mem_attend(qm, mk_t, mv_t, rows):
    sc, masks = _mem_scores(qm, mk_t, rows)
    return _mem_values(sc, masks, mv_t, rows)


def _in_proj_kernel(*refs, tm, s, has_carry, do_mem):
    it = iter(refs)
    (x_ref, cosq_ref, sinq_ref, tabk_ref, npm_ref, win_ref, nq_ref, wuq_ref, nkv_ref,
     wuk_ref, cw_ref, ng_ref) = [next(it) for _ in range(12)]
    c0_ref = next(it) if has_carry else None
    if do_mem:
        mk_ref, mv_ref = next(it), next(it)
    qlat_ref, qpe_ref, kall_ref, ckvt_ref, ckv_ref, kpe_ref = [next(it) for _ in range(6)]
    if do_mem:
        mbc_ref = next(it)
    else:
        mb_ref, qm_ref = next(it), next(it)
    cnew_ref = next(it)
    ubuf = next(it)

    i = pl.program_id(1)
    base = _carry_base(s)
    rs = tm // IN_SUBTILES

    @pl.when(i == 0)
    def _():
        if has_carry:
            ubuf[base - 2 * s:base, :] = c0_ref[0]
        else:
            ubuf[0:base, :] = jnp.zeros((base, CONV_W), F32)

    ng = ng_ref[0]
    w = cw_ref[0]

    def project(t):
        rows = slice(t * rs, (t + 1) * rs)
        h = _rms(x_ref[rows, :], npm_ref[0]).astype(BF16)
        return _dot(h, win_ref[0])

    def mix(t, z):
        rows = slice(t * rs, (t + 1) * rs)
        cqn = _rms(z[:, Z_CQ:Z_CKV], nq_ref[0]).astype(BF16)
        q = _dot(cqn, wuq_ref[0])
        qpe = q[:, 512:768] * cosq_ref[rows, :] + q[:, 768:1024] * sinq_ref[rows, :]
        qpe_ref[0, rows, :] = qpe.astype(BF16)
        for j in range(MLA_HEADS // 2):
            ql = _dot(q[:, 128 * j:128 * (j + 1)].astype(BF16), wuk_ref[0, j])
            qlat_ref[0, 2 * j, rows, :] = ql[:, :KV_LORA].astype(BF16)
            qlat_ref[0, 2 * j + 1, rows, :] = ql[:, KV_LORA:].astype(BF16)

        ckv = _rms(z[:, Z_CKV:Z_GB], nkv_ref[0])
        ckv_ref[rows, :] = ckv
        ckvt_ref[0, :, rows] = ckv.T.astype(BF16)
        tk = z[:, Z_KPE:Z_COLS] * tabk_ref[rows, :]
        r = tk + pltpu.roll(tk, 96, 1)
        lane = lax.broadcasted_iota(jnp.int32, (rs, 128), 1)
        kp = jnp.where(lane < MLA_ROPE, r, 0.0)
        kpe_ref[0, :, rows] = kp.T[0:MLA_ROPE, :]
        kp = kp + pltpu.roll(kp, 32, 1)
        kp = (kp + pltpu.roll(kp, 64, 1)).astype(BF16)
        kall_ref[0, rows, 0:KV_LORA] = ckv.astype(BF16)
        kall_ref[0, rows, KV_LORA:KV_LORA + 128] = kp
        kall_ref[0, rows, KV_LORA + 128:KV_LORA + 256] = kp

        u = z[:, Z_GC:Z_HV] * z[:, Z_HV:Z_QM]
        r0 = base + t * rs
        ubuf[r0:r0 + rs, :] = u
        cv = (w[0:1] * ubuf[r0 - 2 * s:r0 - 2 * s + rs, :]
              + w[1:2] * ubuf[r0 - s:r0 - s + rs, :] + w[2:3] * u)
        mb = _rms(z[:, Z_GB:Z_GC] * cv, ng[:, MLA_W:MLA_W + CONV_W]).astype(BF16)

        qm = z[:, Z_QM:Z_KPE]
        if do_mem:
            mbc_ref[rows, 0:CONV_W] = mb
            return _mem_scores(qm, mk_ref[0, 0], rs)
        mb_ref[rows, :] = mb
        qm_ref[rows, :] = qm
        return None

    def mem_finish(t, st):
        if do_mem:
            rows = slice(t * rs, (t + 1) * rs)
            om = _mem_values(st[0], st[1], mv_ref[0, 0], rs)
            mbc_ref[rows, CONV_W:] = _rms(om, ng[:, MLA_W + CONV_W:]).astype(BF16)

    z = project(0)
    for t in range(IN_SUBTILES):
        st = mix(t, z)
        if t + 1 < IN_SUBTILES:
            z = project(t + 1)
        mem_finish(t, st)
    newc = ubuf[base + tm - 2 * s:base + tm, :]
    ubuf[base - 2 * s:base, :] = newc
    cnew_ref[0] = newc


def _in_proj(l, x, nb, t_len, tm, s, tabs, wts, carry0=None, mem_kv=None):
    nt = t_len // tm
    rows = nb * t_len
    has_carry = carry0 is not None
    do_mem = mem_kv is not None
    row_map = lambda b, i: (b * nt + i, 0)
    tab_map = lambda b, i: (i, 0)
    lay = lambda b, i: (l, 0, 0)
    in_specs = [
        pl.BlockSpec((tm, D_MODEL), row_map),
        pl.BlockSpec((tm, 256), tab_map),
        pl.BlockSpec((tm, 256), tab_map),
        pl.BlockSpec((tm, 128), tab_map),
        pl.BlockSpec((1, 1, D_MODEL), lay),
        _resident((1, D_MODEL, Z_COLS), lay),
        pl.BlockSpec((1, 1, Q_LORA), lay),
        _resident((1, Q_LORA, 1024), lay),
        pl.BlockSpec((1, 1, KV_LORA), lay),
        _resident((1, MLA_HEADS // 2, 128, 2 * KV_LORA), lambda b, i: (l, 0, 0, 0)),
        pl.BlockSpec((1, 3, CONV_W), lay),
        pl.BlockSpec((1, 1, D_MODEL), lay),
    ]
    args = [x, *tabs, wts["norm_pre_mix"], wts["w_in"], wts["norm_q"], wts["w_uq"],
            wts["norm_kv"], wts["w_uk2"], wts["conv_w"], wts["norm_group"]]
    if has_carry:
        in_specs.append(pl.BlockSpec((1, 2 * s, CONV_W), lambda b, i: (b, 0, 0)))
        args.append(carry0)
    if do_mem:
        mspec = pl.BlockSpec((1, 1, MEM_W, MEM_TOKENS), lambda b, i: (l, b, 0, 0))
        in_specs += [mspec, mspec]
        args += list(mem_kv)
    out_shape = [
        jax.ShapeDtypeStruct((nb, MLA_HEADS, t_len, KV_LORA), BF16),
        jax.ShapeDtypeStruct((nb, t_len, 256), BF16),
        jax.ShapeDtypeStruct((nb, t_len, 512), BF16),
        jax.ShapeDtypeStruct((nb, KV_LORA, t_len), BF16),
        jax.ShapeDtypeStruct((rows, KV_LORA), F32),
        jax.ShapeDtypeStruct((nb, MLA_ROPE, t_len), F32),
    ]
    out_specs = [
        pl.BlockSpec((1, MLA_HEADS, tm, KV_LORA), lambda b, i: (b, 0, i, 0)),
        pl.BlockSpec((1, tm, 256), lambda b, i: (b, i, 0)),
        pl.BlockSpec((1, tm, 512), lambda b, i: (b, i, 0)),
        pl.BlockSpec((1, KV_LORA, tm), lambda b, i: (b, 0, i)),
        pl.BlockSpec((tm, KV_LORA), row_map),
        pl.BlockSpec((1, MLA_ROPE, tm), lambda b, i: (b, 0, i)),
    ]
    if do_mem:
        out_shape.append(jax.ShapeDtypeStruct((rows, 512), BF16))
        out_specs.append(pl.BlockSpec((tm, 512), row_map))
    else:
        out_shape += [jax.ShapeDtypeStruct((rows, CONV_W), BF16),
                      jax.ShapeDtypeStruct((rows, MEM_W), F32)]
        out_specs += [pl.BlockSpec((tm, CONV_W), row_map), pl.BlockSpec((tm, MEM_W), row_map)]
    out_shape.append(jax.ShapeDtypeStruct((nb, 2 * s, CONV_W), F32))
    out_specs.append(pl.BlockSpec((1, 2 * s, CONV_W), lambda b, i: (b, 0, 0)))
    kern = functools.partial(_in_proj_kernel, tm=tm, s=s, has_carry=has_carry, do_mem=do_mem)
    return pl.pallas_call(
        kern,
        grid=(nb, nt),
        in_specs=in_specs,
        out_specs=out_specs,
        out_shape=out_shape,
        scratch_shapes=[pltpu.VMEM((_carry_base(s) + tm, CONV_W), F32)],
        compiler_params=_params("parallel", "arbitrary"),
        name="in_proj_mem" if do_mem else "in_proj",
    )(*args)


def _causal_pairs(nq, tq, tk):
    it, kt = [], []
    for i in range(nq):
        for k in range((i * tq + tq - 1) // tk + 1):
            it.append(i)
            kt.append(k)
    return np.asarray(it, np.int32), np.asarray(kt, np.int32)


def _attn_kernel(it_ref, kt_ref, qlat_ref, qpe_ref, kall_ref, ckvt_ref, wuvt_ref, ngb_ref, o_ref,
                 q2, m_scr, l_scr, acct, *, tq, tk):
    assert tq == tk == 2 * ATTN_HALF
    j = pl.program_id(1)
    i = it_ref[j]
    k = kt_ref[j]
    hc = MLA_HEADS * ATTN_HALF
    cols = 2 * hc
    c_exp = MLA_SCALE * LOG2E

    @pl.when(k == 0)
    def _():
        lane = lax.broadcasted_iota(jnp.int32, (ATTN_HALF, 256), 1)
        for th in range(2):
            tr = slice(th * ATTN_HALF, (th + 1) * ATTN_HALF)
            qp = qpe_ref[0, tr, :]
            zero = jnp.zeros_like(qp)
            for h in range(MLA_HEADS):
                r0 = th * hc + h * ATTN_HALF
                q2[r0:r0 + ATTN_HALF, 0:KV_LORA] = qlat_ref[0, h, tr, :]
                q2[r0:r0 + ATTN_HALF, KV_LORA:] = jnp.where(
                    _head_mask(lane, h, MLA_ROPE), qp, zero)
        m_scr[...] = jnp.full((1, cols), NEG, F32)
        l_scr[...] = jnp.zeros((1, cols), F32)
        acct[...] = jnp.zeros((KV_LORA, cols), F32)

    def step(sc, cs, vt):
        m_prev = m_scr[:, cs]
        m_new = jnp.maximum(m_prev, jnp.max(sc, axis=0, keepdims=True))
        alpha = jnp.exp2((m_prev - m_new) * c_exp)
        p = jnp.exp2((sc - m_new) * c_exp)
        l_scr[:, cs] = alpha * l_scr[:, cs] + jnp.sum(p, axis=0, keepdims=True)
        acct[:, cs] = alpha * acct[:, cs] + _dot(vt, p.astype(BF16))
        m_scr[:, cs] = m_new

    @pl.when(k < i)
    def _():
        step(_dot_nt(kall_ref[0], q2[...]), slice(0, cols), ckvt_ref[0])

    @pl.when(k == i)
    def _():
        first = slice(0, ATTN_HALF)
        second = slice(ATTN_HALF, 2 * ATTN_HALF)
        row = lax.broadcasted_iota(jnp.int32, (ATTN_HALF, cols), 0)
        col = lax.broadcasted_iota(jnp.int32, (ATTN_HALF, cols), 1)
        ok = (col >= hc) | (row <= (col & (ATTN_HALF - 1)))
        sc = _dot_nt(kall_ref[0, first, :], q2[...])
        step(jnp.where(ok, sc, NEG), slice(0, cols), ckvt_ref[0, :, first])
        row = lax.broadcasted_iota(jnp.int32, (ATTN_HALF, hc), 0)
        col = lax.broadcasted_iota(jnp.int32, (ATTN_HALF, hc), 1)
        sc = _dot_nt(kall_ref[0, second, :], q2[hc:cols, :])
        step(jnp.where(row <= (col & (ATTN_HALF - 1)), sc, NEG), slice(hc, cols),
             ckvt_ref[0, :, second])

        ot = (acct[...] * (1.0 / l_scr[...])).astype(BF16)
        omt = jnp.concatenate([
            jnp.concatenate([
                _dot(wuvt_ref[0, h], ot[:, th * hc + h * ATTN_HALF:th * hc + (h + 1) * ATTN_HALF])
                for h in range(MLA_HEADS)], axis=0)
            for th in range(2)], axis=1)
        y = omt * lax.rsqrt(jnp.mean(omt * omt, axis=0, keepdims=True) + EPS) * ngb_ref[0]
        o_ref[...] = y.T.astype(BF16)


def _prompt_attention(l, qlat, qpe, kall, ckvt, wts):
    nb, _, t_len, _ = qlat.shape
    tq, tk = ATTN_TQ, ATTN_TK
    nq = t_len // tq
    cols = MLA_HEADS * tq
    it, kt = _causal_pairs(nq, tq, tk)
    kern = functools.partial(_attn_kernel, tq=tq, tk=tk)
    grid_spec = pltpu.PrefetchScalarGridSpec(
        num_scalar_prefetch=2,
        grid=(nb, it.shape[0]),
        in_specs=[
            pl.BlockSpec((1, MLA_HEADS, tq, KV_LORA), lambda b, j, it, kt: (b, 0, it[j], 0)),
            pl.BlockSpec((1, tq, 256), lambda b, j, it, kt: (b, it[j], 0)),
            pl.BlockSpec((1, tk, 512), lambda b, j, it, kt: (b, kt[j], 0)),
            pl.BlockSpec((1, KV_LORA, tk), lambda b, j, it, kt: (b, 0, kt[j])),
            _resident((1, MLA_HEADS, MLA_V, KV_LORA), lambda b, j, it, kt: (l, 0, 0, 0)),
            pl.BlockSpec((1, MLA_W, tq), lambda b, j, it, kt: (l, 0, 0)),
        ],
        out_specs=pl.BlockSpec((tq, MLA_W), lambda b, j, it, kt: (b * nq + it[j], 0)),
        scratch_shapes=[
            pltpu.VMEM((cols, 512), BF16),
            pltpu.VMEM((1, cols), F32),
            pltpu.VMEM((1, cols), F32),
            pltpu.VMEM((KV_LORA, cols), F32),
        ],
    )
    return pl.pallas_call(
        kern,
        grid_spec=grid_spec,
        out_shape=jax.ShapeDtypeStruct((nb * t_len, MLA_W), BF16),
        compiler_params=_params("parallel", "arbitrary"),
        name="prompt_attn",
    )(jnp.asarray(it), jnp.asarray(kt), qlat, qpe, kall, ckvt, wts["w_uv_t"], wts["ng_attn"])


def _sample_attn_kernel(pt_ref, qlat_ref, qpe_ref, cnew_ref, pnew_ref, ckv_hbm, kpe_hbm, o_ref,
                        kbuf, pbuf, kb16, s_scr, sem, *, l, n_pages, page):
    b = pl.program_id(0)
    nb = pl.num_programs(0)
    slot = b % 2
    past = n_pages * page

    def page_copies(seq, sl, p):
        pg = pt_ref[seq, p]
        dst = pl.ds(pl.multiple_of(p * page, page), page)
        return (pltpu.make_async_copy(ckv_hbm.at[l, pg], kbuf.at[sl, dst, :], sem.at[0, sl]),
                pltpu.make_async_copy(kpe_hbm.at[l, pg], pbuf.at[sl, :, dst], sem.at[1, sl]))

    def issue(seq, sl):
        for p in range(n_pages):
            for cp in page_copies(seq, sl, p):
                cp.start(priority=p % 2)

    @pl.when(b == 0)
    def _():
        issue(0, 0)

    @pl.when(b + 1 < nb)
    def _():
        issue(b + 1, 1 - slot)

    first = pl.ds(0, page)
    for _ in range(n_pages):
        pltpu.make_async_copy(ckv_hbm.at[l, 0], kbuf.at[slot, first, :], sem.at[0, slot]).wait()
    for _ in range(n_pages):
        pltpu.make_async_copy(kpe_hbm.at[l, 0], pbuf.at[slot, :, first], sem.at[1, slot]).wait()

    q = qlat_ref[0]
    qp = qpe_ref[0]
    nrow = q.shape[0]
    nch = past // PAGE_CHUNK
    for c in range(nch):
        sl_c = slice(c * PAGE_CHUNK, (c + 1) * PAGE_CHUNK)
        kb16[sl_c, :] = kbuf[slot, sl_c, :].astype(BF16)
    s_scr[...] = (_dot_nt(q, kb16[...]) + _dot(qp, pbuf[slot].astype(BF16))) * MLA_SCALE

    @pl.when(b >= 0)
    def _():
        cn = cnew_ref[0].astype(BF16)
        pn = pnew_ref[0].astype(BF16)
        s_new = (_dot_nt(q, cn) + _dot(qp, pn)) * MLA_SCALE
        t_row = lax.broadcasted_iota(jnp.int32, (nrow, NEW_PAD), 0) // MLA_HEADS
        j_col = lax.broadcasted_iota(jnp.int32, (nrow, NEW_PAD), 1)
        s_new = jnp.where(j_col <= t_row, s_new, NEG)
        s_all = s_scr[...]
        m = jnp.maximum(jnp.max(s_all, axis=-1, keepdims=True),
                        jnp.max(s_new, axis=-1, keepdims=True))
        p_new = jnp.exp(s_new - m)
        p_all = jnp.exp(s_all - m)
        denom = jnp.sum(p_all, axis=-1, keepdims=True) + jnp.sum(p_new, axis=-1, keepdims=True)
        s_scr[...] = p_all
        accs = [_dot(p_new.astype(BF16), cn), None]
        for c in range(nch):
            sl_c = slice(c * PAGE_CHUNK, (c + 1) * PAGE_CHUNK)
            d = _dot(s_scr[:, sl_c].astype(BF16), kb16[sl_c, :])
            accs[c % 2] = d if accs[c % 2] is None else accs[c % 2] + d
        acc = accs[0] if accs[1] is None else accs[0] + accs[1]
        o_ref[0] = acc / denom


def _sample_attention(l, page_table, qlat, qpe, cnew, pnew_t, cache_ckv, cache_kpe_t):
    nseq, nrow, _ = qlat.shape
    n_pages = page_table.shape[1]
    page = cache_ckv.shape[2]
    past = n_pages * page
    kern = functools.partial(_sample_attn_kernel, l=l, n_pages=n_pages, page=page)
    grid_spec = pltpu.PrefetchScalarGridSpec(
        num_scalar_prefetch=1,
        grid=(nseq,),
        in_specs=[
            pl.BlockSpec((1, nrow, KV_LORA), lambda b, pt: (b, 0, 0)),
            pl.BlockSpec((1, nrow, MLA_ROPE), lambda b, pt: (b, 0, 0)),
            pl.BlockSpec((1, NEW_PAD, KV_LORA), lambda b, pt: (b, 0, 0)),
            pl.BlockSpec((1, MLA_ROPE, NEW_PAD), lambda b, pt: (b, 0, 0)),
            pl.BlockSpec(memory_space=pl.ANY),
            pl.BlockSpec(memory_space=pl.ANY),
        ],
        out_specs=pl.BlockSpec((1, nrow, KV_LORA), lambda b, pt: (b, 0, 0)),
        scratch_shapes=[
            pltpu.VMEM((2, past, KV_LORA), F32),
            pltpu.VMEM((2, MLA_ROPE, past), F32),
            pltpu.VMEM((past, KV_LORA), BF16),
            pltpu.VMEM((nrow, past), F32),
            pltpu.SemaphoreType.DMA((2, 2)),
        ],
    )
    return pl.pallas_call(
        kern,
        grid_spec=grid_spec,
        out_shape=jax.ShapeDtypeStruct((nseq, nrow, KV_LORA), F32),
        compiler_params=_params("arbitrary"),
        name="sample_attn",
    )(page_table, qlat, qpe, cnew, pnew_t, cache_ckv, cache_kpe_t)


def _uv_kernel(o_ref, w_ref, ng_ref, out_ref):
    om = _dot(o_ref[...].astype(BF16), w_ref[0])
    out_ref[...] = _rms(om, ng_ref[0][:, 0:MLA_W]).astype(BF16)


def _uv_proj(l, o_lat, wts):
    rows = o_lat.shape[0]
    return pl.pallas_call(
        _uv_kernel,
        grid=(1,),
        in_specs=[
            pl.BlockSpec((rows, MLA_HEADS * KV_LORA), lambda i: (0, 0)),
            pl.BlockSpec((1, MLA_HEADS * KV_LORA, MLA_W), lambda i: (l, 0, 0)),
            pl.BlockSpec((1, 1, D_MODEL), lambda i: (l, 0, 0)),
        ],
        out_specs=pl.BlockSpec((rows, MLA_W), lambda i: (0, 0)),
        out_shape=jax.ShapeDtypeStruct((rows, MLA_W), BF16),
        compiler_params=_params("arbitrary"),
        name="uv_proj",
    )(o_lat, wts["w_uv_flat"], wts["norm_group"])


def _sample_mem_kernel(q_ref, mk_ref, mv_ref, ng_ref, o_ref):
    ng = ng_ref[0]
    for j in range(SMEM_BLOCK):
        om = _mem_attend(q_ref[j], mk_ref[0, j], mv_ref[0, j], SUBLANES)
        o_ref[j] = _rms(om, ng[:, MLA_W + CONV_W:])


def _sample_mem(l, q, mem_k, mem_v, wts):
    nseq = q.shape[0]
    mspec = pl.BlockSpec((1, SMEM_BLOCK, MEM_W, MEM_TOKENS), lambda j: (l, j, 0, 0))
    return pl.pallas_call(
        _sample_mem_kernel,
        grid=(nseq // SMEM_BLOCK,),
        in_specs=[
            pl.BlockSpec((SMEM_BLOCK, SUBLANES, MEM_W), lambda j: (j, 0, 0)),
            mspec, mspec,
            pl.BlockSpec((1, 1, D_MODEL), lambda j: (l, 0, 0)),
        ],
        out_specs=pl.BlockSpec((SMEM_BLOCK, SUBLANES, MEM_W), lambda j: (j, 0, 0)),
        out_shape=jax.ShapeDtypeStruct((nseq, SUBLANES, MEM_W), F32),
        compiler_params=_params("parallel"),
        name="sample_mem",
    )(q, mem_k, mem_v, wts["norm_group"])


def _ffn_kernel(*refs, tm, s, has_carry, widths):
    it = iter(refs)
    x_ref = next(it)
    piece_refs = [next(it) for _ in widths]
    (wout_ref, npost_ref, npre_ref, wg_ref, wu_ref, fcw_ref, fcb_ref, wd_ref,
     npf_ref) = [next(it) for _ in range(9)]
    c0_ref = next(it) if has_carry else None
    y_ref, fnew_ref, gbuf = next(it), next(it), next(it)

    i = pl.program_id(1)
    base = _carry_base(s)
    rs = tm // FFN_SUBTILES
    nchunk = D_FF // FF_CHUNK
    cols = lambda c: slice(c * FF_CHUNK, (c + 1) * FF_CHUNK)

    @pl.when(i == 0)
    def _():
        if has_carry:
            gbuf[base - 2 * s:base, :] = c0_ref[0]
        else:
            gbuf[0:base, :] = jnp.zeros((base, D_FF), F32)

    w = fcw_ref[0]
    bias = fcb_ref[0]

    def prologue(t):
        rows = slice(t * rs, (t + 1) * rs)
        att = None
        off = 0
        for pref, wd in zip(piece_refs, widths):
            part = _dot(pref[rows, :], wout_ref[0, off:off + wd, :])
            att = part if att is None else att + part
            off += wd
        x1 = x_ref[rows, :] + _rms(att, npost_ref[0])
        return x1, _rms(x1, npre_ref[0]).astype(BF16)

    def gate_up(h2, c):
        return _dot(h2, wg_ref[0, :, cols(c)]), _dot(h2, wu_ref[0, :, cols(c)])

    def act(t, c, g, up):
        r0 = base + t * rs
        cs = cols(c)
        gbuf[r0:r0 + rs, cs] = g
        gc = (w[0:1, cs] * gbuf[r0 - 2 * s:r0 - 2 * s + rs, cs]
              + w[1:2, cs] * gbuf[r0 - s:r0 - s + rs, cs] + w[2:3, cs] * g + bias[:, cs])
        return (gc * (1.0 / (1.0 + jnp.exp(-gc))) * up).astype(BF16)

    x1s, h2s = [None] * FFN_SUBTILES, [None] * FFN_SUBTILES
    x1s[0], h2s[0] = prologue(0)
    items = [(t, c) for t in range(FFN_SUBTILES) for c in range(nchunk)]
    nxt = gate_up(h2s[0], 0)
    dn = None
    for idx, (t, c) in enumerate(items):
        g, up = nxt
        if idx == 0:
            for t2 in range(1, FFN_SUBTILES):
                x1s[t2], h2s[t2] = prologue(t2)
        if idx + 1 < len(items):
            t_n, c_n = items[idx + 1]
            nxt = gate_up(h2s[t_n], c_n)
        part = _dot(act(t, c, g, up), wd_ref[0, cols(c), :])
        dn = part if c == 0 else dn + part
        if c == nchunk - 1:
            y_ref[t * rs:(t + 1) * rs, :] = x1s[t] + _rms(dn, npf_ref[0])
    newc = gbuf[base + tm - 2 * s:base + tm, :]
    gbuf[base - 2 * s:base, :] = newc
    fnew_ref[0] = newc


def _out_ffn(l, x, pieces, nb, t_len, tm, s, wts, carry0=None):
    nt = t_len // tm
    rows = nb * t_len
    has_carry = carry0 is not None
    widths = tuple(p.shape[1] for p in pieces)
    row_map = lambda b, i: (b * nt + i, 0)
    lay = lambda b, i: (l, 0, 0)
    in_specs = [pl.BlockSpec((tm, D_MODEL), row_map)]
    in_specs += [pl.BlockSpec((tm, wd), row_map) for wd in widths]
    in_specs += [
        _resident((1, D_MODEL, D_MODEL), lay),
        pl.BlockSpec((1, 1, D_MODEL), lay),
        pl.BlockSpec((1, 1, D_MODEL), lay),
        _resident((1, D_MODEL, D_FF), lay),
        _resident((1, D_MODEL, D_FF), lay),
        pl.BlockSpec((1, 3, D_FF), lay),
        pl.BlockSpec((1, 1, D_FF), lay),
        _resident((1, D_FF, D_MODEL), lay),
        pl.BlockSpec((1, 1, D_MODEL), lay),
    ]
    args = [x, *pieces, wts["w_out"], wts["norm_post_mix"], wts["norm_pre_ffn"], wts["w_gate"],
            wts["w_up"], wts["ffn_conv_w"], wts["ffn_conv_b"], wts["w_down"],
            wts["norm_post_ffn"]]
    if has_carry:
        in_specs.append(pl.BlockSpec((1, 2 * s, D_FF), lambda b, i: (b, 0, 0)))
        args.append(carry0)
    kern = functools.partial(_ffn_kernel, tm=tm, s=s, has_carry=has_carry, widths=widths)
    return pl.pallas_call(
        kern,
        grid=(nb, nt),
        in_specs=in_specs,
        out_specs=[pl.BlockSpec((tm, D_MODEL), row_map),
                   pl.BlockSpec((1, 2 * s, D_FF), lambda b, i: (b, 0, 0))],
        out_shape=[jax.ShapeDtypeStruct((rows, D_MODEL), F32),
                   jax.ShapeDtypeStruct((nb, 2 * s, D_FF), F32)],
        scratch_shapes=[pltpu.VMEM((_carry_base(s) + tm, D_FF), F32)],
        compiler_params=_params("parallel", "arbitrary"),
        name="out_ffn",
    )(*args)


def _rot_pair(w):
    half = MLA_ROPE // 2
    return jnp.concatenate([-w[..., half:], w[..., :half]], axis=-1)


def _prep_weights(norm_pre_mix, w_in, norm_q, w_uq, norm_kv, w_uk, w_uv, conv_w, norm_mem,
                  w_mem_k, w_mem_v, norm_group, w_out, norm_post_mix, norm_pre_ffn, w_gate, w_up,
                  ffn_conv_w, ffn_conv_b, w_down, norm_post_ffn):
    o0 = Q_LORA
    o1 = o0 + KV_LORA
    o2 = o1 + MLA_ROPE
    o3 = o2 + CONV_W
    o4 = o3 + CONV_W
    o5 = o4 + CONV_W
    kpe_w = w_in[..., o1:o2]
    w_in_r = jnp.concatenate(
        [w_in[..., :o1], w_in[..., o2:], kpe_w, _rot_pair(kpe_w),
         jnp.zeros(w_in.shape[:2] + (128 - 2 * MLA_ROPE,), w_in.dtype)], axis=-1).astype(BF16)

    per_head = w_uq.reshape(DEPTH, Q_LORA, MLA_HEADS, MLA_NOPE + MLA_ROPE)
    nope = per_head[..., :MLA_NOPE].reshape(DEPTH, Q_LORA, MLA_HEADS * MLA_NOPE)
    pe = per_head[..., MLA_NOPE:]
    w_uq_r = jnp.concatenate(
        [nope, pe.reshape(DEPTH, Q_LORA, -1), _rot_pair(pe).reshape(DEPTH, Q_LORA, -1)],
        axis=-1).astype(BF16)

    wt = jnp.transpose(w_uk, (0, 2, 3, 1)).reshape(DEPTH, MLA_HEADS // 2, 2, MLA_NOPE, KV_LORA)
    w_uk2 = jnp.einsum("ljanr,ab->ljanbr", wt, jnp.eye(2, dtype=wt.dtype)).reshape(
        DEPTH, MLA_HEADS // 2, 2 * MLA_NOPE, 2 * KV_LORA).astype(BF16)

    w_uv_exp = jnp.einsum("lrhv,hg->lhrgv", w_uv, jnp.eye(MLA_HEADS, dtype=w_uv.dtype)).reshape(
        DEPTH, MLA_HEADS, KV_LORA, MLA_W).astype(BF16)

    vec = lambda a: a.reshape(DEPTH, 1, a.shape[-1])
    return dict(
        norm_pre_mix=vec(norm_pre_mix), w_in=w_in_r, norm_q=vec(norm_q), w_uq=w_uq_r,
        norm_kv=vec(norm_kv), w_uk2=w_uk2,
        w_uv_t=jnp.transpose(w_uv, (0, 2, 3, 1)).astype(BF16),
        ng_attn=jnp.broadcast_to(norm_group[:, :MLA_W, None], (DEPTH, MLA_W, ATTN_TQ)),
        w_uv_flat=w_uv_exp.reshape(DEPTH, MLA_HEADS * KV_LORA, MLA_W), conv_w=conv_w,
        norm_mem=vec(norm_mem), w_mem_k=jnp.transpose(w_mem_k, (0, 2, 1)).astype(BF16),
        w_mem_v=jnp.transpose(w_mem_v, (0, 2, 1)).astype(BF16),
        norm_group=vec(norm_group), w_out=w_out.astype(BF16), norm_post_mix=vec(norm_post_mix),
        norm_pre_ffn=vec(norm_pre_ffn), w_gate=w_gate.astype(BF16), w_up=w_up.astype(BF16),
        ffn_conv_w=ffn_conv_w, ffn_conv_b=vec(ffn_conv_b), w_down=w_down.astype(BF16),
        norm_post_ffn=vec(norm_post_ffn))


def _rope_tables(pos):
    half = MLA_ROPE // 2
    freqs = ROPE_THETA ** (-jnp.arange(half, dtype=F32) * (2.0 / MLA_ROPE))
    ang = pos.astype(F32)[:, None] * freqs[None, :]
    cos = jnp.cos(ang)
    sin = jnp.sin(ang)
    cos32 = jnp.concatenate([cos, cos], axis=-1)
    sin32 = jnp.concatenate([sin, sin], axis=-1)
    tabk = jnp.concatenate([cos32, sin32, jnp.zeros((pos.shape[0], 64), F32)], axis=-1)
    return jnp.tile(cos32, (1, MLA_HEADS)), jnp.tile(sin32, (1, MLA_HEADS)), tabk


def _to_time_major(a):
    return jnp.transpose(a, (1, 0, 2)).reshape(-1, a.shape[-1])


def _from_time_major(a, t_len):
    return jnp.transpose(a.reshape(t_len, -1, a.shape[-1]), (1, 0, 2))


def kernel(x_prompt, x_sample, mem_prompt, cache_ckv, cache_kpe, cache_mem_k, cache_mem_v, state_conv, state_ffn_conv, page_table, norm_pre_mix, w_in, norm_q, w_uq, norm_kv, w_uk, w_uv, conv_w, norm_mem, w_mem_k, w_mem_v, norm_group, w_out, norm_post_mix, norm_pre_ffn, w_gate, w_up, ffn_conv_w, ffn_conv_b, w_down, norm_post_ffn):
    nb, seq, _ = x_prompt.shape
    nseq, dseq, _ = x_sample.shape
    n_pages = page_table.shape[1]
    page = cache_ckv.shape[2]
    past = n_pages * page
    wts = _prep_weights(norm_pre_mix, w_in, norm_q, w_uq, norm_kv, w_uk, w_uv, conv_w, norm_mem,
                        w_mem_k, w_mem_v, norm_group, w_out, norm_post_mix, norm_pre_ffn, w_gate,
                        w_up, ffn_conv_w, ffn_conv_b, w_down, norm_post_ffn)

    tabs_p = _rope_tables(jnp.arange(seq, dtype=jnp.int32))
    pos_s = jnp.repeat(past + jnp.arange(dseq, dtype=jnp.int32), nseq)
    tabs_s = _rope_tables(pos_s)

    mem_k_all, mem_v_all = _mem_kv(mem_prompt, wts["norm_mem"], wts["w_mem_k"], wts["w_mem_v"])
    smem_k = jnp.transpose(cache_mem_k, (0, 1, 3, 4, 2)).reshape(DEPTH, nseq, MEM_W, MEM_TOKENS)
    smem_v = jnp.transpose(cache_mem_v, (0, 1, 3, 4, 2)).reshape(DEPTH, nseq, MEM_W, MEM_TOKENS)
    cache_kpe_t = jnp.transpose(cache_kpe, (0, 1, 3, 2))

    yp = x_prompt.reshape(nb * seq, D_MODEL)
    ys = _to_time_major(x_sample)
    srows = nseq * dseq
    p_ckv, p_kpe, p_conv, p_ffn = [], [], [], []
    s_ckv, s_kpe, s_conv, s_ffn = [], [], [], []
    for l in range(DEPTH):
        qlat, qpe, kall, ckvt, ckv, kpt, mbc, cnew = _in_proj(
            l, yp, nb, seq, IN_TM, 1, tabs_p, wts,
            mem_kv=(mem_k_all, mem_v_all))
        ma = _prompt_attention(l, qlat, qpe, kall, ckvt, wts)
        yp, fnew = _out_ffn(l, yp, [ma, mbc], nb, seq, FFN_TM, 1, wts)
        p_ckv.append(ckv.reshape(nb, seq, KV_LORA))
        p_kpe.append(kpt)
        p_conv.append(cnew)
        p_ffn.append(fnew)

        c0 = _to_time_major(state_conv[l])[None]
        f0 = _to_time_major(state_ffn_conv[l])[None]
        qlat, qpe, _, _, ckv, kpt, mb, qm, cnew = _in_proj(
            l, ys, 1, srows, srows, nseq, tabs_s, wts, carry0=c0)
        ckv_bt = _from_time_major(ckv, dseq)
        kp_dtb = kpt[0].reshape(MLA_ROPE, dseq, nseq)
        qlat_bt = jnp.transpose(qlat[0].reshape(MLA_HEADS, dseq, nseq, KV_LORA),
                                (2, 1, 0, 3)).reshape(nseq, dseq * MLA_HEADS, KV_LORA)
        qpe_bt = jnp.transpose(qpe[0].reshape(dseq, nseq, MLA_HEADS, MLA_ROPE),
                               (1, 0, 2, 3)).reshape(nseq, dseq * MLA_HEADS, MLA_ROPE)
        cnew16 = jnp.pad(ckv_bt, ((0, 0), (0, NEW_PAD - dseq), (0, 0)))
        pnew16 = jnp.pad(jnp.transpose(kp_dtb, (2, 0, 1)), ((0, 0), (0, 0), (0, NEW_PAD - dseq)))
        o_lat = _sample_attention(l, page_table, qlat_bt, qpe_bt, cnew16, pnew16,
                                  cache_ckv, cache_kpe_t)
        o_lat = _to_time_major(o_lat.reshape(nseq, dseq, MLA_HEADS * KV_LORA))
        ma = _uv_proj(l, o_lat, wts)
        qm_bt = jnp.pad(_from_time_major(qm, dseq), ((0, 0), (0, SUBLANES - dseq), (0, 0)))
        mc = _sample_mem(l, qm_bt, smem_k, smem_v, wts)
        mc = _to_time_major(mc[:, :dseq]).astype(BF16)
        ys, fnew = _out_ffn(l, ys, [ma, mb, mc], 1, srows, srows, nseq, wts, carry0=f0)
        s_ckv.append(ckv_bt)
        s_kpe.append(jnp.transpose(kp_dtb, (2, 1, 0)))
        s_conv.append(_from_time_major(cnew[0], 2))
        s_ffn.append(_from_time_major(fnew[0], 2))

    def mem_out(a):
        return jnp.transpose(a.reshape(DEPTH, nb, MEM_HEADS, HEAD_DIM, MEM_TOKENS), (0, 1, 4, 2, 3))

    return (yp.reshape(nb, seq, D_MODEL), _from_time_major(ys, dseq),
            jnp.stack(p_ckv), jnp.transpose(jnp.stack(p_kpe), (0, 1, 3, 2)),
            mem_out(mem_k_all), mem_out(mem_v_all),
            jnp.stack(p_conv), jnp.stack(p_ffn),
            jnp.stack(s_ckv), jnp.stack(s_kpe), jnp.stack(s_conv), jnp.stack(s_ffn))
```
